```python
import math
import jax, jax.numpy as jnp
from jax import lax
import numpy as np

D_MODEL = 2048
BATCH = 4
SEQ = 2048
DEPTH = 4
DEC_BATCH = 8
DEC_SEQ = 8
PAST_LEN = 16384
PAGE_SIZE = 128

MIX_WIDTH = D_MODEL
SB_HEAD_DIM = 64
SB_WIDTH = MIX_WIDTH // 4
SB_HEADS = SB_WIDTH // SB_HEAD_DIM
SB_BLOCK = 128
SB_BIAS_NEAR = -4.0
SB_BIAS_FAR = -10.0
SSD_HEAD_DIM = 64
SSD_WIDTH = MIX_WIDTH // 2
SSD_HEADS = SSD_WIDTH // SSD_HEAD_DIM
SSD_GROUPS = 2
SSD_STATE = 128
SSD_CONV = 4
SSD_CHUNK = 128
SSD_CONV_DIM = SSD_WIDTH + 2 * SSD_GROUPS * SSD_STATE
POOL_WIDTH = MIX_WIDTH - SB_WIDTH - SSD_WIDTH
POOL_WINDOWS = (2, 4, 8, 16)
POOL_GROUPS = len(POOL_WINDOWS)
POOL_GROUP_DIM = POOL_WIDTH // POOL_GROUPS
POOL_HIST = max(POOL_WINDOWS) - 1
D_FF = ((8 * D_MODEL // 3 + 255) // 256) * 256
FFN_CONV = 3
RMS_EPS = 1e-6

SPLITS = (SB_WIDTH, 2 * SB_WIDTH, 3 * SB_WIDTH,
          3 * SB_WIDTH + SSD_WIDTH,
          3 * SB_WIDTH + SSD_WIDTH + SSD_CONV_DIM,
          3 * SB_WIDTH + SSD_WIDTH + SSD_CONV_DIM + SSD_HEADS)
IN_DIM = 3 * SB_WIDTH + SSD_WIDTH + SSD_CONV_DIM + SSD_HEADS + POOL_WIDTH

kernel_name = 'hybrid_stickbreak_ssd_pool_convffn_step'


def rms_norm(x, gain):
    xf = x.astype(jnp.float32)
    y = xf * lax.rsqrt(jnp.mean(xf * xf, axis=-1, keepdims=True) + RMS_EPS)
    return (y * gain.astype(jnp.float32)).astype(x.dtype)


def causal_dwconv(hist, x, w, bias):
    width = w.shape[0]
    L = x.shape[1]
    xh = jnp.concatenate([hist.astype(x.dtype), x], axis=1)
    out = bias + xh[:, 0:L] * w[0]
    for k in range(1, width):
        out = out + xh[:, k:k + L] * w[k]
    return out, xh[:, L:]


def stick_breaking_attention(q, k, v, pos0, head_bias):
    b, L, H, Dh = q.shape
    Tk = k.shape[1]
    scale = 1.0 / math.sqrt(Dh)
    blk = SB_BLOCK if L % SB_BLOCK == 0 else L
    nb = L // blk
    k_pos = jnp.arange(Tk)
    q_blocks = q.reshape(b, nb, blk, H, Dh).swapaxes(0, 1)
    q_pos = (pos0 + jnp.arange(L)).reshape(nb, blk)
    hb = head_bias.astype(jnp.float32)[None, :, None, None]

    def one_block(args):
        qi, pi = args
        z = jnp.einsum('bqhd,bkhd->bhqk', qi, k, preferred_element_type=jnp.float32) * scale + hb
        mask = k_pos[None, :] < pi[:, None]
        log_keep = jnp.where(mask, jax.nn.log_sigmoid(-z), 0.0)
        later = lax.cumsum(log_keep, axis=3, reverse=True) - log_keep
        w = jnp.where(mask, jnp.exp(jax.nn.log_sigmoid(z) + later), 0.0)
        return jnp.einsum('bhqk,bkhd->bqhd', w.astype(v.dtype), v)

    o = lax.map(one_block, (q_blocks, q_pos))
    return o.swapaxes(0, 1).reshape(b, L, H * Dh)


def ssd_chunked(x, dt, a, bm, cm, s0):
    b, L, H, P = x.shape
    G, N = bm.shape[2], bm.shape[3]
    E = H // G
    Q = SSD_CHUNK if L % SSD_CHUNK == 0 else L
    c = L // Q
    xc = x.reshape(b, c, Q, G, E, P)
    dtc = dt.reshape(b, c, Q, G, E)
    bc = bm.reshape(b, c, Q, G, N)
    cc = cm.reshape(b, c, Q, G, N)
    cum = jnp.cumsum(dtc * a.reshape(G, E), axis=2)
    causal = jnp.tril(jnp.ones((Q, Q), dtype=bool))[:, :, None, None]
    seg = cum[:, :, :, None] - cum[:, :, None, :]
    decay = jnp.exp(jnp.where(causal, seg, -jnp.inf))
    cb = jnp.einsum('bctgn,bcsgn->bctsg', cc, bc)
    w_ts = cb[..., None] * decay * dtc[:, :, None]
    y = jnp.einsum('bctsge,bcsgep->bctgep', w_ts, xc)
    to_end = jnp.exp(cum[:, :, -1:] - cum) * dtc
    chunk_states = jnp.einsum('bcsgn,bcsge,bcsgep->bcgepn', bc, to_end, xc)
    chunk_decay = jnp.exp(cum[:, :, -1])

    def step(s, inp):
        dec, st = inp
        return dec[..., None, None] * s + st, s

    s_final, s_in = lax.scan(step, s0.reshape(b, G, E, P, N),
                             (chunk_decay.swapaxes(0, 1), chunk_states.swapaxes(0, 1)))
    s_in = s_in.swapaxes(0, 1)
    y = y + jnp.einsum('bctgn,bcgepn,bctge->bctgep', cc, s_in, jnp.exp(cum))
    return y.reshape(b, L, H, P), s_final.reshape(b, H, P, N)


def multiscale_pool(xp, hist, pos0, pool_w, pool_scale):
    b, L, C = xp.shape
    xh = jnp.concatenate([hist.astype(xp.dtype), xp], axis=1)
    xf = xh.astype(jnp.float32)
    cs = jnp.concatenate([jnp.zeros((b, 1, C), jnp.float32), jnp.cumsum(xf, axis=1)], axis=1)
    pos = pos0 + jnp.arange(L)
    pooled = []
    for g, win in enumerate(POOL_WINDOWS):
        ch = slice(g * POOL_GROUP_DIM, (g + 1) * POOL_GROUP_DIM)
        wsum = cs[:, POOL_HIST + 1:POOL_HIST + 1 + L, ch] - cs[:, POOL_HIST + 1 - win:POOL_HIST + 1 - win + L, ch]
        count = jnp.minimum(win, pos + 1).astype(jnp.float32)
        pooled.append(wsum / count[None, :, None])
    diff = jnp.concatenate(pooled, axis=-1) - xf[:, POOL_HIST:]
    diff = diff.reshape(b, L, POOL_GROUPS, POOL_GROUP_DIM)
    y = jnp.einsum('blgc,gcd->blgd', diff, pool_w).reshape(b, L, C) * pool_scale
    return y.astype(xp.dtype), xh[:, L:]


def hybrid_layer(x, l, pos0, past_k, past_v, ssm0, conv_hist, pool_hist, ffn_hist, weights):
    (norm1, w_in, sb_bias, ssd_conv_w, ssd_conv_b, dt_bias, a_log, d_skip, ssd_norm,
     pool_w, pool_scale, w_out, norm2, w_gate, w_up, ffn_conv_w, ffn_conv_b, w_down) = weights
    b, L, _ = x.shape
    h = rms_norm(x, norm1[l])
    proj = h @ w_in[l]
    q, k, v, z, xbc, dt_raw, xp = jnp.split(proj, SPLITS, axis=-1)

    q = q.reshape(b, L, SB_HEADS, SB_HEAD_DIM)
    k = k.reshape(b, L, SB_HEADS, SB_HEAD_DIM)
    v = v.reshape(b, L, SB_HEADS, SB_HEAD_DIM)
    k_all = k if past_k is None else jnp.concatenate([past_k.astype(k.dtype), k], axis=1)
    v_all = v if past_v is None else jnp.concatenate([past_v.astype(v.dtype), v], axis=1)
    o_sb = stick_breaking_attention(q, k_all, v_all, pos0, sb_bias[l]).astype(x.dtype)

    xbc_c, conv_new = causal_dwconv(conv_hist, xbc, ssd_conv_w[l], ssd_conv_b[l])
    xbc_c = jax.nn.silu(xbc_c)
    xs, bm, cm = jnp.split(xbc_c, (SSD_WIDTH, SSD_WIDTH + SSD_GROUPS * SSD_STATE), axis=-1)
    xs4 = xs.reshape(b, L, SSD_HEADS, SSD_HEAD_DIM).astype(jnp.float32)
    dt = jax.nn.softplus(dt_raw.astype(jnp.float32) + dt_bias[l].astype(jnp.float32))
    a = -jnp.exp(a_log[l].astype(jnp.float32))
    y_ssd, ssm_new = ssd_chunked(xs4, dt, a,
                                 bm.reshape(b, L, SSD_GROUPS, SSD_STATE).astype(jnp.float32),
                                 cm.reshape(b, L, SSD_GROUPS, SSD_STATE).astype(jnp.float32),
                                 ssm0.astype(jnp.float32))
    y_ssd = (y_ssd + d_skip[l].astype(jnp.float32)[:, None] * xs4).reshape(b, L, SSD_WIDTH)
    y_ssd = rms_norm(y_ssd * jax.nn.silu(z.astype(jnp.float32)), ssd_norm[l]).astype(x.dtype)

    y_pool, pool_new = multiscale_pool(xp, pool_hist, pos0, pool_w[l], pool_scale[l])

    x = x + jnp.concatenate([o_sb, y_ssd, y_pool], axis=-1) @ w_out[l]

    h2 = rms_norm(x, norm2[l])
    g = h2 @ w_gate[l]
    u = h2 @ w_up[l]
    g_c, ffn_new = causal_dwconv(ffn_hist, g, ffn_conv_w[l], ffn_conv_b[l])
    x = x + (jax.nn.silu(g_c) * u) @ w_down[l]
    return x, k, v, ssm_new, conv_new, pool_new, ffn_new


def run_trunk(x, pos0, cache_k, cache_v, page_table, st_ssm, st_conv, st_pool, st_ffn, weights, final_norm):
    b = x.shape[0]
    new_rows = [[] for _ in range(6)]
    for l in range(DEPTH):
        if cache_k is None:
            past_k = past_v = None
            ssm0 = jnp.zeros((b, SSD_HEADS, SSD_HEAD_DIM, SSD_STATE), jnp.float32)
            conv_hist = jnp.zeros((b, SSD_CONV - 1, SSD_CONV_DIM), x.dtype)
            pool_hist = jnp.zeros((b, POOL_HIST, POOL_WIDTH), x.dtype)
            ffn_hist = jnp.zeros((b, FFN_CONV - 1, D_FF), x.dtype)
        else:
            past_k = cache_k[l][page_table].reshape(b, -1, SB_HEADS, SB_HEAD_DIM)
            past_v = cache_v[l][page_table].reshape(b, -1, SB_HEADS, SB_HEAD_DIM)
            ssm0, conv_hist, pool_hist, ffn_hist = st_ssm[l], st_conv[l], st_pool[l], st_ffn[l]
        x, k_new, v_new, ssm_new, conv_new, pool_new, ffn_new = hybrid_layer(
            x, l, pos0, past_k, past_v, ssm0, conv_hist, pool_hist, ffn_hist, weights)
        for lst, arr in zip(new_rows, (k_new, v_new, ssm_new, conv_new, pool_new, ffn_new)):
            lst.append(arr)
    stacked = [jnp.stack(lst) for lst in new_rows]
    return rms_norm(x, final_norm), stacked


def setup_inputs(seed: int = 0) -> dict:
    key = jax.random.key(seed)
    ks = jax.random.split(key, 32)
    f32 = jnp.float32
    n_pages = PAST_LEN // PAGE_SIZE
    n_used = DEC_BATCH * n_pages
    n_pool = n_used + max(1, n_used // 4)
    out_scale = (2 * DEPTH) ** -0.5

    def nrm(k, shape, s=1.0):
        return jax.random.normal(k, shape, f32) * s

    dt0 = jnp.exp(jax.random.uniform(ks[14], (DEPTH, SSD_HEADS), f32) * (math.log(0.1) - math.log(0.001)) + math.log(0.001))
    sb_bias0 = jnp.linspace(SB_BIAS_NEAR, SB_BIAS_FAR, SB_HEADS, dtype=f32)[None, :]
    inputs = {
        'x_prompt': nrm(ks[0], (BATCH, SEQ, D_MODEL)),
        'x_sample': nrm(ks[1], (DEC_BATCH, DEC_SEQ, D_MODEL)),
        'cache_k': nrm(ks[2], (DEPTH, n_pool, PAGE_SIZE, SB_HEADS, SB_HEAD_DIM)),
        'cache_v': nrm(ks[3], (DEPTH, n_pool, PAGE_SIZE, SB_HEADS, SB_HEAD_DIM)),
        'page_table': jax.random.permutation(ks[4], n_pool)[:n_used].reshape(DEC_BATCH, n_pages).astype(jnp.int32),
        'state_ssm': nrm(ks[5], (DEPTH, DEC_BATCH, SSD_HEADS, SSD_HEAD_DIM, SSD_STATE), 0.1),
        'state_ssm_conv': nrm(ks[6], (DEPTH, DEC_BATCH, SSD_CONV - 1, SSD_CONV_DIM)),
        'state_pool': nrm(ks[7], (DEPTH, DEC_BATCH, POOL_HIST, POOL_WIDTH)),
        'state_ffn_conv': nrm(ks[8], (DEPTH, DEC_BATCH, FFN_CONV - 1, D_FF)),
        'norm1': 1.0 + nrm(ks[9], (DEPTH, D_MODEL), 0.02),
        'w_in': nrm(ks[10], (DEPTH, D_MODEL, IN_DIM), D_MODEL ** -0.5),
        'sb_bias': sb_bias0 + nrm(ks[28], (DEPTH, SB_HEADS), 0.1),
        'ssd_conv_w': nrm(ks[11], (DEPTH, SSD_CONV, SSD_CONV_DIM), SSD_CONV ** -0.5),
        'ssd_conv_b': nrm(ks[12], (DEPTH, SSD_CONV_DIM), 0.02),
        'dt_bias': dt0 + jnp.log(-jnp.expm1(-dt0)),
        'a_log': jnp.log(jax.random.uniform(ks[15], (DEPTH, SSD_HEADS), f32, 1.0, 16.0)),
        'd_skip': 1.0 + nrm(ks[16], (DEPTH, SSD_HEADS), 0.1),
        'ssd_norm': 1.0 + nrm(ks[17], (DEPTH, SSD_WIDTH), 0.02),
        'pool_w': nrm(ks[18], (DEPTH, POOL_GROUPS, POOL_GROUP_DIM, POOL_GROUP_DIM), POOL_GROUP_DIM ** -0.5),
        'pool_scale': 1.0 + nrm(ks[19], (DEPTH, POOL_WIDTH), 0.02),
        'w_out': nrm(ks[20], (DEPTH, MIX_WIDTH, D_MODEL), MIX_WIDTH ** -0.5 * out_scale),
        'norm2': 1.0 + nrm(ks[21], (DEPTH, D_MODEL), 0.02),
        'w_gate': nrm(ks[22], (DEPTH, D_MODEL, D_FF), D_MODEL ** -0.5),
        'w_up': nrm(ks[23], (DEPTH, D_MODEL, D_FF), D_MODEL ** -0.5),
        'ffn_conv_w': nrm(ks[24], (DEPTH, FFN_CONV, D_FF), FFN_CONV ** -0.5),
        'ffn_conv_b': nrm(ks[25], (DEPTH, D_FF), 0.02),
        'w_down': nrm(ks[26], (DEPTH, D_FF, D_MODEL), D_FF ** -0.5 * out_scale),
        'final_norm': 1.0 + nrm(ks[27], (D_MODEL,), 0.02),
    }
    return inputs


def reference(x_prompt, x_sample, cache_k, cache_v, page_table, state_ssm, state_ssm_conv, state_pool,
              state_ffn_conv, norm1, w_in, sb_bias, ssd_conv_w, ssd_conv_b, dt_bias, a_log, d_skip, ssd_norm,
              pool_w, pool_scale, w_out, norm2, w_gate, w_up, ffn_conv_w, ffn_conv_b, w_down, final_norm):
    weights = (norm1, w_in, sb_bias, ssd_conv_w, ssd_conv_b, dt_bias, a_log, d_skip, ssd_norm,
               pool_w, pool_scale, w_out, norm2, w_gate, w_up, ffn_conv_w, ffn_conv_b, w_down)
    y_prompt, (k_p, v_p, ssm_p, conv_p, pool_p, ffn_p) = run_trunk(
        x_prompt, 0, None, None, None, None, None, None, None, weights, final_norm)
    y_sample, (k_s, v_s, ssm_s, conv_s, pool_s, ffn_s) = run_trunk(
        x_sample, PAST_LEN, cache_k, cache_v, page_table, state_ssm, state_ssm_conv, state_pool,
        state_ffn_conv, weights, final_norm)
    return (y_prompt, y_sample, k_p, v_p, ssm_p, conv_p, pool_p, ffn_p,
            k_s, v_s, ssm_s, conv_s, pool_s, ffn_s)
```

```python
import functools
import math

import jax
import jax.numpy as jnp
from jax import lax
from jax.experimental import pallas as pl
from jax.experimental.pallas import tpu as pltpu

F32 = jnp.float32
BF16 = jnp.bfloat16

RMS_EPS = 1e-6
SB_HEADS = 8
SB_HEAD_DIM = 64
SB_WIDTH = SB_HEADS * SB_HEAD_DIM
SSD_HEADS = 16
SSD_HEAD_DIM = 64
SSD_WIDTH = SSD_HEADS * SSD_HEAD_DIM
SSD_GROUPS = 2
SSD_STATE = 128
SSD_CONV = 4
SSD_CHUNK = 128
SSD_CONV_DIM = SSD_WIDTH + 2 * SSD_GROUPS * SSD_STATE
POOL_WINDOWS = (2, 4, 8, 16)
POOL_GROUP_DIM = 128
POOL_WIDTH = POOL_GROUP_DIM * len(POOL_WINDOWS)
POOL_HIST = max(POOL_WINDOWS) - 1
FFN_CONV = 3

LANES = 128
SUBLANES = 8
VMEM_LIMIT = 56 * 1024 * 1024

COL_XBC = 0
COL_Q = SSD_CONV_DIM
COL_K = COL_Q + SB_WIDTH
COL_V = COL_K + SB_WIDTH
COL_Z = COL_V + SB_WIDTH
COL_POOL = COL_Z + SSD_WIDTH
PROJ_WIDTH = COL_POOL + POOL_WIDTH


def _cparams(sem):
    return pltpu.CompilerParams(dimension_semantics=sem, vmem_limit_bytes=VMEM_LIMIT)


def _softplus(x):
    return jnp.maximum(x, 0.0) + jnp.log1p(jnp.exp(-jnp.abs(x)))


def _silu(x):
    return x * jax.nn.sigmoid(x)


def _split_bf16(x, n):
    parts = []
    r = x
    for i in range(n):
        p = r.astype(BF16)
        parts.append(p)
        if i + 1 < n:
            r = r - p.astype(F32)
    return parts


def _dot_sel_rhs(x, sel, n):
    acc = None
    for p in _split_bf16(x, n):
        d = jnp.dot(p, sel, preferred_element_type=F32)
        acc = d if acc is None else acc + d
    return acc


def _dot_sel_lhs(sel, x, n):
    acc = None
    for p in _split_bf16(x, n):
        d = jnp.dot(sel, p, preferred_element_type=F32)
        acc = d if acc is None else acc + d
    return acc


def _dot_nt(a, b):
    return lax.dot_general(a, b, (((1,), (1,)), ((), ())), preferred_element_type=F32)


def _pad_rows(v, rows):
    if v.shape[0] == rows:
        return v
    return jnp.concatenate([v, jnp.zeros((rows - v.shape[0], v.shape[1]), v.dtype)], axis=0)


def _tail_rows(hist, x, n):
    seq = x.shape[1]
    if seq >= n:
        return x[:, seq - n:]
    return jnp.concatenate([hist[:, seq:], x], axis=1)


def _rms(x, gain):
    return x * lax.rsqrt(jnp.mean(x * x, axis=-1, keepdims=True) + RMS_EPS) * gain


def _norm_kernel(x_ref, g_ref, o_ref):
    o_ref[...] = _rms(x_ref[...], g_ref[...]).astype(o_ref.dtype)


def _rmsnorm(x, gain, tm, out_dtype):
    m, d = x.shape
    return pl.pallas_call(
        _norm_kernel,
        grid=(m // tm,),
        in_specs=[pl.BlockSpec((tm, d), lambda i: (i, 0)),
                  pl.BlockSpec((1, d), lambda i: (0, 0))],
        out_specs=pl.BlockSpec((tm, d), lambda i: (i, 0)),
        out_shape=jax.ShapeDtypeStruct((m, d), out_dtype),
        compiler_params=_cparams(("parallel",)),
        name="rmsnorm",
    )(x, gain.reshape(1, d))


def _inproj_kernel(h_ref, w_ref, wdt_ref, o_ref, dt_ref):
    h = h_ref[...]
    o_ref[...] = jnp.dot(h, w_ref[...], preferred_element_type=F32)

    @pl.when(pl.program_id(1) == 0)
    def _():
        dt_ref[...] = jnp.dot(h, wdt_ref[...], preferred_element_type=F32)


def _inproj(h, w, wdt, tm, tn):
    m, d = h.shape
    n = w.shape[1]
    return pl.pallas_call(
        _inproj_kernel,
        grid=(m // tm, n // tn),
        in_specs=[pl.BlockSpec((tm, d), lambda i, j: (i, 0)),
                  pl.BlockSpec((d, tn), lambda i, j: (0, j)),
                  pl.BlockSpec((d, LANES), lambda i, j: (0, 0))],
        out_specs=[pl.BlockSpec((tm, tn), lambda i, j: (i, j)),
                   pl.BlockSpec((tm, LANES), lambda i, j: (i, 0))],
        out_shape=[jax.ShapeDtypeStruct((m, n), F32),
                   jax.ShapeDtypeStruct((m, LANES), F32)],
        compiler_params=_cparams(("parallel", "arbitrary")),
        name="inproj",
    )(h, w, wdt)


def _sb_tile(z, car, tri, mask):
    sp = _softplus(z)
    if mask is not None:
        sp = jnp.where(mask, sp, 0.0)
    later = _dot_sel_rhs(sp, tri, 2)
    w = jnp.exp(z - sp - later - car)
    if mask is not None:
        w = jnp.where(mask, w, 0.0)
    return w, jnp.sum(sp, axis=-1, keepdims=True)


def _attn_prompt_kernel(bias_ref, q_ref, k_ref, v_ref, tri_ref, o_ref, *, tq):
    i = pl.program_id(1)
    lane = lax.broadcasted_iota(jnp.int32, (1, LANES), 1)
    first_head = lane < SB_HEAD_DIM
    row = lax.broadcasted_iota(jnp.int32, (tq, tq), 0)
    col = lax.broadcasted_iota(jnp.int32, (tq, tq), 1)
    diag_mask = col < row
    tri = tri_ref[...]
    scale = 1.0 / math.sqrt(SB_HEAD_DIM)

    for p in range(SB_HEADS // 2):
        cols = slice(LANES * p, LANES * (p + 1))
        q2 = q_ref[:, cols] * scale
        qa = jnp.where(first_head, q2, 0.0).astype(BF16)
        qb = jnp.where(first_head, 0.0, q2).astype(BF16)
        ba = bias_ref[2 * p]
        bb = bias_ref[2 * p + 1]

        def tile(kb, carry, mask):
            acc_a, acc_b, car_a, car_b = carry
            start = pl.multiple_of(kb * tq, tq)
            k2 = k_ref[pl.ds(start, tq), cols].astype(BF16)
            v2 = v_ref[pl.ds(start, tq), cols].astype(BF16)
            wa, sa = _sb_tile(_dot_nt(qa, k2) + ba, car_a, tri, mask)
            wb, sb = _sb_tile(_dot_nt(qb, k2) + bb, car_b, tri, mask)
            acc_a = acc_a + jnp.dot(wa.astype(BF16), v2, preferred_element_type=F32)
            acc_b = acc_b + jnp.dot(wb.astype(BF16), v2, preferred_element_type=F32)
            return acc_a, acc_b, car_a + sa, car_b + sb

        zacc = jnp.zeros((tq, LANES), F32)
        zcar = jnp.zeros((tq, 1), F32)
        carry = tile(i, (zacc, zacc, zcar, zcar), diag_mask)
        carry = lax.fori_loop(0, i, lambda t, c: tile(i - 1 - t, c, None), carry)
        o_ref[:, cols] = jnp.where(first_head, carry[0], carry[1]).astype(o_ref.dtype)


def _attn_prompt(proj, bias, batch, seq, tq):
    m = proj.shape[0]
    nq = seq // tq
    blk = SB_WIDTH
    tri = (lax.broadcasted_iota(jnp.int32, (tq, tq), 0)
           > lax.broadcasted_iota(jnp.int32, (tq, tq), 1)).astype(BF16)
    return pl.pallas_call(
        functools.partial(_attn_prompt_kernel, tq=tq),
        grid=(batch, nq),
        in_specs=[pl.BlockSpec(memory_space=pltpu.SMEM),
                  pl.BlockSpec((tq, blk), lambda b, i: (b * nq + i, COL_Q // blk)),
                  pl.BlockSpec((seq, blk), lambda b, i: (b, COL_K // blk)),
                  pl.BlockSpec((seq, blk), lambda b, i: (b, COL_V // blk)),
                  pl.BlockSpec((tq, tq), lambda b, i: (0, 0))],
        out_specs=pl.BlockSpec((tq, blk), lambda b, i: (b * nq + i, 0)),
        out_shape=jax.ShapeDtypeStruct((m, blk), BF16),
        compiler_params=_cparams(("parallel", "arbitrary")),
        name="attn_prompt",
    )(bias, proj, proj, proj, tri)


def _same_head(rows, n_new):
    rh = lax.broadcasted_iota(jnp.int32, (rows, SB_WIDTH), 0) >> (n_new.bit_length() - 1)
    ch = lax.broadcasted_iota(jnp.int32, (rows, SB_WIDTH), 1) >> (SB_HEAD_DIM.bit_length() - 1)
    return rh == ch


def _attn_sample_kernel(pt_ref, brow_ref, q_ref, kn_ref, vn_ref, tri_ref, *rest, pages_per_step, n_new):
    k_refs = rest[:pages_per_step]
    v_refs = rest[pages_per_step:2 * pages_per_step]
    o_ref = rest[2 * pages_per_step]
    qbd_ref, acc_ref, car_ref = rest[2 * pages_per_step + 1:]
    s = pl.program_id(1)
    rows = SB_HEADS * n_new
    page = tri_ref.shape[0]
    tri = tri_ref[...]
    brow = brow_ref[...]

    def tile(k_page, v_page, mask):
        z = _dot_nt(qbd_ref[...], k_page) + brow
        w, ssum = _sb_tile(z, car_ref[...], tri, mask)
        acc_ref[...] += jnp.dot(w.astype(BF16), v_page, preferred_element_type=F32)
        car_ref[...] += ssum

    @pl.when(s == 0)
    def _():
        q = q_ref[...] * (1.0 / math.sqrt(SB_HEAD_DIM))
        qt = jnp.concatenate([q] * SB_HEADS, axis=0)
        qbd_ref[...] = jnp.where(_same_head(rows, n_new), qt, 0.0).astype(BF16)
        acc_ref[...] = jnp.zeros_like(acc_ref)
        car_ref[...] = jnp.zeros_like(car_ref)
        kn = _pad_rows(kn_ref[...], page).astype(BF16)
        vn = _pad_rows(vn_ref[...], page).astype(BF16)
        qi = lax.broadcasted_iota(jnp.int32, (rows, page), 0) & (n_new - 1)
        kj = lax.broadcasted_iota(jnp.int32, (rows, page), 1)
        tile(kn, vn, kj < qi)

    for j in range(pages_per_step):
        tile(k_refs[j][...].astype(BF16), v_refs[j][...].astype(BF16), None)

    @pl.when(s == pl.num_programs(1) - 1)
    def _():
        acc = jnp.where(_same_head(rows, n_new), acc_ref[...], 0.0)
        out = acc[0:n_new]
        for h in range(1, SB_HEADS):
            out = out + acc[h * n_new:(h + 1) * n_new]
        o_ref[...] = out.astype(o_ref.dtype)


def _attn_sample(proj, bias, cache_k, cache_v, page_table, layer, batch, n_new, pages_per_step):
    page = cache_k.shape[2]
    n_pages = page_table.shape[0] // batch
    n_steps = n_pages // pages_per_step
    rows = SB_HEADS * n_new
    blk = SB_WIDTH
    tri = (lax.broadcasted_iota(jnp.int32, (page, page), 0)
           > lax.broadcasted_iota(jnp.int32, (page, page), 1)).astype(BF16)
    brow = jnp.broadcast_to(jnp.repeat(bias, n_new)[:, None], (rows, page))

    def page_map(j):
        def index_map(b, s, pt):
            return (layer, pt[b * n_pages + (n_pages - 1 - (s * pages_per_step + j))], 0, 0)
        return index_map

    page_specs = [pl.BlockSpec((None, None, page, blk), page_map(j)) for j in range(pages_per_step)]
    grid_spec = pltpu.PrefetchScalarGridSpec(
        num_scalar_prefetch=1,
        grid=(batch, n_steps),
        in_specs=[pl.BlockSpec((rows, page), lambda b, s, pt: (0, 0)),
                  pl.BlockSpec((n_new, blk), lambda b, s, pt: (b, COL_Q // blk)),
                  pl.BlockSpec((n_new, blk), lambda b, s, pt: (b, COL_K // blk)),
                  pl.BlockSpec((n_new, blk), lambda b, s, pt: (b, COL_V // blk)),
                  pl.BlockSpec((page, page), lambda b, s, pt: (0, 0))]
                 + page_specs + page_specs,
        out_specs=pl.BlockSpec((n_new, blk), lambda b, s, pt: (b, 0)),
        scratch_shapes=[pltpu.VMEM((rows, blk), BF16),
                        pltpu.VMEM((rows, blk), F32),
                        pltpu.VMEM((rows, page), F32)],
    )
    return pl.pallas_call(
        functools.partial(_attn_sample_kernel, pages_per_step=pages_per_step, n_new=n_new),
        grid_spec=grid_spec,
        out_shape=jax.ShapeDtypeStruct((batch * n_new, blk), F32),
        compiler_params=_cparams(("parallel", "arbitrary")),
        name="attn_sample",
    )(page_table, brow, proj, proj, proj, tri,
      *([cache_k] * pages_per_step), *([cache_v] * pages_per_step))


def _ssd_kernel(xbc_ref, z_ref, dt_ref, hist_ref, s0_ref, cw_ref, cb_ref, dtb_ref, alog_ref,
                dskip_ref, gain_ref, tri_ref, eexp_ref, y_ref, snew_ref, xh_scr, st_scr,
                *, rows_in):
    q = SSD_CHUNK
    c = pl.program_id(1)
    n_state = SSD_STATE
    gw = SSD_WIDTH // SSD_GROUPS

    @pl.when(c == 0)
    def _():
        xh_scr[0:SUBLANES, :] = hist_ref[...]
        for j in range(SSD_WIDTH // LANES):
            st_scr[:, LANES * j:LANES * (j + 1)] = s0_ref[LANES * j:LANES * (j + 1), :].T

    x = _pad_rows(xbc_ref[...], q)
    xh_scr[SUBLANES:SUBLANES + q, :] = x
    cw = cw_ref[...]
    conv = cb_ref[...] + x * cw[SSD_CONV - 1:SSD_CONV]
    for k in range(1, SSD_CONV):
        conv = conv + xh_scr[SUBLANES - k:SUBLANES - k + q, :] * cw[SSD_CONV - 1 - k:SSD_CONV - k]
    xh_scr[0:SUBLANES, :] = xh_scr[q:q + SUBLANES, :]
    xc = _silu(conv)
    xs = xc[:, :SSD_WIDTH]

    dt = _softplus(_pad_rows(dt_ref[...], q) + dtb_ref[...])
    if rows_in < q:
        rvalid = lax.broadcasted_iota(jnp.int32, (q, LANES), 0) < rows_in
        dt = jnp.where(rvalid, dt, 0.0)
    a = -jnp.exp(alog_ref[...])
    tri = tri_ref[...]
    cum = _dot_sel_lhs(tri, dt * a, 3)
    cum_t = cum.T
    dt_t = dt.T
    eexp = eexp_ref[...]
    cum_x = _dot_sel_rhs(cum, eexp, 3)
    dt_x = _dot_sel_rhs(dt, eexp, 3)
    cend_x = cum_x[q - 1:q, :]
    to_end_x = jnp.exp(cend_x - cum_x) * dt_x
    xw = (xs * to_end_x).astype(BF16)

    tt = lax.broadcasted_iota(jnp.int32, (q, q), 0)
    ss = lax.broadcasted_iota(jnp.int32, (q, q), 1)
    causal = ss <= tt
    lane = lax.broadcasted_iota(jnp.int32, (1, LANES), 1)
    first_head = lane < SSD_HEAD_DIM

    pairs_per_group = gw // LANES
    y_groups = []
    for g in range(SSD_GROUPS):
        bg = xc[:, SSD_WIDTH + g * n_state:SSD_WIDTH + (g + 1) * n_state]
        cg = xc[:, SSD_WIDTH + (SSD_GROUPS + g) * n_state:SSD_WIDTH + (SSD_GROUPS + g + 1) * n_state]
        cg16 = cg.astype(BF16)
        gcols = slice(gw * g, gw * (g + 1))
        y_off = jnp.dot(cg16, st_scr[:, gcols].astype(BF16), preferred_element_type=F32)
        cb = _dot_nt(cg16, bg.astype(BF16))
        y_pairs = []
        for jj in range(pairs_per_group):
            j = g * pairs_per_group + jj
            xs2 = xs[:, LANES * j:LANES * (j + 1)].astype(BF16)
            halves = []
            for hl in range(2):
                h = 2 * j + hl
                seg = cum[:, h:h + 1] - cum_t[h:h + 1, :]
                decay = jnp.exp(jnp.where(causal, seg, -1e30))
                wts = cb * decay * dt_t[h:h + 1, :]
                halves.append(jnp.dot(wts.astype(BF16), xs2, preferred_element_type=F32))
            y_pairs.append(jnp.where(first_head, halves[0], halves[1]))
        y_groups.append(jnp.concatenate(y_pairs, axis=1) + y_off * jnp.exp(cum_x[:, gcols]))
        st_scr[:, gcols] = (st_scr[:, gcols] * jnp.exp(cend_x[:, gcols])
                            + jnp.dot(bg.T.astype(BF16), xw[:, gcols], preferred_element_type=F32))

    y = jnp.concatenate(y_groups, axis=1) + dskip_ref[...] * xs
    y = _rms(y * _silu(_pad_rows(z_ref[...], q)), gain_ref[...])
    y_ref[...] = y[0:rows_in].astype(y_ref.dtype)

    @pl.when(c == pl.num_programs(1) - 1)
    def _():
        for j in range(SSD_WIDTH // LANES):
            snew_ref[LANES * j:LANES * (j + 1), :] = st_scr[:, LANES * j:LANES * (j + 1)].T


def _ssd(proj, dt_raw, hist, s0, conv_w, conv_b, dt_bias, a_log, d_skip, gain, batch, seq, out_dtype):
    q = SSD_CHUNK
    m = proj.shape[0]
    rows_in = q if seq % q == 0 else seq
    nc = seq // rows_in
    tri = (lax.broadcasted_iota(jnp.int32, (q, q), 1)
           <= lax.broadcasted_iota(jnp.int32, (q, q), 0)).astype(BF16)
    eexp = (lax.broadcasted_iota(jnp.int32, (LANES, SSD_WIDTH), 0)
            == lax.broadcasted_iota(jnp.int32, (LANES, SSD_WIDTH), 1) // SSD_HEAD_DIM).astype(BF16)
    pad_h = LANES - SSD_HEADS
    cw = jnp.pad(conv_w, ((0, SUBLANES - SSD_CONV), (0, 0)))
    const = lambda shape: pl.BlockSpec(shape, lambda b, c: (0,) * len(shape))
    return pl.pallas_call(
        functools.partial(_ssd_kernel, rows_in=rows_in),
        grid=(batch, nc),
        in_specs=[pl.BlockSpec((rows_in, SSD_CONV_DIM), lambda b, c: (b * nc + c, COL_XBC // SSD_CONV_DIM)),
                  pl.BlockSpec((rows_in, SSD_WIDTH), lambda b, c: (b * nc + c, COL_Z // SSD_WIDTH)),
                  pl.BlockSpec((rows_in, LANES), lambda b, c: (b * nc + c, 0)),
                  pl.BlockSpec((None, SUBLANES, SSD_CONV_DIM), lambda b, c: (b, 0, 0)),
                  pl.BlockSpec((None, SSD_WIDTH, SSD_STATE), lambda b, c: (b, 0, 0)),
                  const((SUBLANES, SSD_CONV_DIM)),
                  const((1, SSD_CONV_DIM)),
                  const((1, LANES)),
                  const((1, LANES)),
                  const((1, SSD_WIDTH)),
                  const((1, SSD_WIDTH)),
                  const((q, q)),
                  const((LANES, SSD_WIDTH))],
        out_specs=[pl.BlockSpec((rows_in, SSD_WIDTH), lambda b, c: (b * nc + c, 0)),
                   pl.BlockSpec((None, SSD_WIDTH, SSD_STATE), lambda b, c: (b, 0, 0))],
        out_shape=[jax.ShapeDtypeStruct((m, SSD_WIDTH), out_dtype),
                   jax.ShapeDtypeStruct((batch, SSD_WIDTH, SSD_STATE), F32)],
        scratch_shapes=[pltpu.VMEM((SUBLANES + q + SUBLANES, SSD_CONV_DIM), F32),
                        pltpu.VMEM((SSD_STATE, SSD_WIDTH), F32)],
        compiler_params=_cparams(("parallel", "arbitrary")),
        name="ssd",
    )(proj, proj, dt_raw, hist, s0, cw, conv_b.reshape(1, -1),
      jnp.pad(dt_bias, (0, pad_h)).reshape(1, LANES), jnp.pad(a_log, (0, pad_h)).reshape(1, LANES),
      jnp.repeat(d_skip, SSD_HEAD_DIM).reshape(1, SSD_WIDTH), gain.reshape(1, SSD_WIDTH), tri, eexp)


def _pool_kernel(x_ref, hist_ref, w_ref, scale_ref, y_ref, buf, *, tm, pos0, carry):
    hist_rows = 2 * SUBLANES
    t = pl.program_id(1)

    @pl.when(t == 0)
    def _():
        buf[0:hist_rows, :] = hist_ref[...]

    x = x_ref[...]
    buf[hist_rows:hist_rows + tm, :] = x
    pos = pos0 + t * tm + lax.broadcasted_iota(jnp.int32, (tm, 1), 0)
    for g, win in enumerate(POOL_WINDOWS):
        cols = slice(POOL_GROUP_DIM * g, POOL_GROUP_DIM * (g + 1))
        xg = x[:, cols]
        wsum = xg
        for k in range(1, win):
            wsum = wsum + buf[hist_rows - k:hist_rows - k + tm, cols]
        count = jnp.minimum(win, pos + 1).astype(F32)
        diff = wsum / count - xg
        yg = jnp.dot(diff.astype(BF16), w_ref[g], preferred_element_type=F32) * scale_ref[:, cols]
        y_ref[:, cols] = yg.astype(y_ref.dtype)
    if carry:
        buf[0:hist_rows, :] = buf[tm:tm + hist_rows, :]


def _pool(proj, hist, w, scale, batch, seq, tm, pos0, out_dtype):
    m = proj.shape[0]
    nt = seq // tm
    blk = POOL_WIDTH
    return pl.pallas_call(
        functools.partial(_pool_kernel, tm=tm, pos0=pos0, carry=nt > 1),
        grid=(batch, nt),
        in_specs=[pl.BlockSpec((tm, blk), lambda b, t: (b * nt + t, COL_POOL // blk)),
                  pl.BlockSpec((None, 2 * SUBLANES, blk), lambda b, t: (b, 0, 0)),
                  pl.BlockSpec((len(POOL_WINDOWS), POOL_GROUP_DIM, POOL_GROUP_DIM), lambda b, t: (0, 0, 0)),
                  pl.BlockSpec((1, blk), lambda b, t: (0, 0))],
        out_specs=pl.BlockSpec((tm, blk), lambda b, t: (b * nt + t, 0)),
        out_shape=jax.ShapeDtypeStruct((m, blk), out_dtype),
        scratch_shapes=[pltpu.VMEM((2 * SUBLANES + tm, blk), F32)],
        compiler_params=_cparams(("parallel", "arbitrary")),
        name="pool",
    )(proj, hist, w, scale.reshape(1, blk))


def _outproj_kernel(o_ref, y_ref, p_ref, x_ref, w_ref, g_ref, xn_ref, h_ref):
    a, b = SB_WIDTH, SB_WIDTH + SSD_WIDTH
    acc = x_ref[...]
    acc = acc + jnp.dot(o_ref[...].astype(BF16), w_ref[0:a, :], preferred_element_type=F32)
    acc = acc + jnp.dot(y_ref[...].astype(BF16), w_ref[a:b, :], preferred_element_type=F32)
    acc = acc + jnp.dot(p_ref[...].astype(BF16), w_ref[b:, :], preferred_element_type=F32)
    xn_ref[...] = acc
    h_ref[...] = _rms(acc, g_ref[...]).astype(h_ref.dtype)


def _outproj(o, y, p, x, w, gain, tm):
    m, d = x.shape
    row = lambda width: pl.BlockSpec((tm, width), lambda i: (i, 0))
    return pl.pallas_call(
        _outproj_kernel,
        grid=(m // tm,),
        in_specs=[row(SB_WIDTH), row(SSD_WIDTH), row(POOL_WIDTH), row(d),
                  pl.BlockSpec(w.shape, lambda i: (0, 0)),
                  pl.BlockSpec((1, d), lambda i: (0, 0))],
        out_specs=[row(d), row(d)],
        out_shape=[jax.ShapeDtypeStruct((m, d), F32), jax.ShapeDtypeStruct((m, d), BF16)],
        compiler_params=_cparams(("parallel",)),
        name="outproj",
    )(o, y, p, x, w, gain.reshape(1, d))


def _ffn_up_kernel(h_ref, wg_ref, wu_ref, hist_ref, cw_ref, cb_ref, act_ref, tail_ref, buf,
                   *, seg_rows, n_seg, tiles_per_seq):
    h = h_ref[...]
    g = jnp.dot(h, wg_ref[...], preferred_element_type=F32)
    u = jnp.dot(h, wu_ref[...], preferred_element_type=F32)
    cw = cw_ref[...]
    cb = cb_ref[...]

    def conv_rows(gs):
        buf[SUBLANES:SUBLANES + seg_rows, :] = gs
        out = cb + gs * cw[FFN_CONV - 1:FFN_CONV]
        for k in range(1, FFN_CONV):
            out = out + buf[SUBLANES - k:SUBLANES - k + seg_rows, :] * cw[FFN_CONV - 1 - k:FFN_CONV - k]
        return out

    if n_seg == 1:
        @pl.when(pl.program_id(1) % tiles_per_seq == 0)
        def _():
            buf[0:SUBLANES, :] = hist_ref[...]
        gc = conv_rows(g)
        buf[0:SUBLANES, :] = buf[seg_rows:seg_rows + SUBLANES, :]
        tail_ref[...] = g[seg_rows - SUBLANES:seg_rows]
        act_ref[...] = (_silu(gc) * u).astype(act_ref.dtype)
    else:
        for s in range(n_seg):
            rows = slice(seg_rows * s, seg_rows * (s + 1))
            buf[0:SUBLANES, :] = hist_ref[s]
            gc = conv_rows(g[rows])
            act_ref[rows, :] = (_silu(gc) * u[rows]).astype(act_ref.dtype)
        tail_ref[...] = g


def _ffn_up(h, wg, wu, hist, conv_w, conv_b, batch, seq, tm, tn):
    m, d = h.shape
    dff = wg.shape[1]
    nt = m // tm
    cw = jnp.pad(conv_w, ((0, SUBLANES - FFN_CONV), (0, 0)))
    if tm <= seq:
        n_seg, seg_rows, tps = 1, tm, seq // tm
        hist_spec = pl.BlockSpec((None, SUBLANES, tn), lambda j, i: (i // tps, 0, j))
        tail_spec = pl.BlockSpec((None, SUBLANES, tn), lambda j, i: (i, 0, j))
        tail_shape = jax.ShapeDtypeStruct((nt, SUBLANES, dff), F32)
    else:
        n_seg, seg_rows, tps = tm // seq, seq, 1
        hist_spec = pl.BlockSpec((n_seg, SUBLANES, tn), lambda j, i: (i, 0, j))
        tail_spec = pl.BlockSpec((tm, tn), lambda j, i: (i, j))
        tail_shape = jax.ShapeDtypeStruct((m, dff), F32)
    return pl.pallas_call(
        functools.partial(_ffn_up_kernel, seg_rows=seg_rows, n_seg=n_seg, tiles_per_seq=tps),
        grid=(dff // tn, nt),
        in_specs=[pl.BlockSpec((tm, d), lambda j, i: (i, 0)),
                  pl.BlockSpec((d, tn), lambda j, i: (0, j)),
                  pl.BlockSpec((d, tn), lambda j, i: (0, j)),
                  hist_spec,
                  pl.BlockSpec((SUBLANES, tn), lambda j, i: (0, j)),
                  pl.BlockSpec((1, tn), lambda j, i: (0, j))],
        out_specs=[pl.BlockSpec((tm, tn), lambda j, i: (i, j)), tail_spec],
        out_shape=[jax.ShapeDtypeStruct((m, dff), BF16), tail_shape],
        scratch_shapes=[pltpu.VMEM((SUBLANES + seg_rows + SUBLANES, tn), F32)],
        compiler_params=_cparams(("parallel", "arbitrary")),
        name="ffn_up",
    )(h, wg, wu, hist, cw, conv_b.reshape(1, dff))


def _ffn_down_kernel(a_ref, w_ref, x_ref, g_ref, xn_ref, h_ref):
    k = pl.program_id(1)
    part = jnp.dot(a_ref[...], w_ref[...], preferred_element_type=F32)

    @pl.when(k == 0)
    def _():
        xn_ref[...] = x_ref[...] + part

    @pl.when(k > 0)
    def _():
        xn_ref[...] += part

    @pl.when(k == pl.num_programs(1) - 1)
    def _():
        h_ref[...] = _rms(xn_ref[...], g_ref[...]).astype(h_ref.dtype)


def _ffn_down(act, w, x, gain, tm, tk, norm_dtype):
    m, d = x.shape
    dff = act.shape[1]
    return pl.pallas_call(
        _ffn_down_kernel,
        grid=(m // tm, dff // tk),
        in_specs=[pl.BlockSpec((tm, tk), lambda i, k: (i, k)),
                  pl.BlockSpec((tk, d), lambda i, k: (k, 0)),
                  pl.BlockSpec((tm, d), lambda i, k: (i, 0)),
                  pl.BlockSpec((1, d), lambda i, k: (0, 0))],
        out_specs=[pl.BlockSpec((tm, d), lambda i, k: (i, 0)),
                   pl.BlockSpec((tm, d), lambda i, k: (i, 0))],
        out_shape=[jax.ShapeDtypeStruct((m, d), F32), jax.ShapeDtypeStruct((m, d), norm_dtype)],
        compiler_params=_cparams(("parallel", "arbitrary")),
        name="ffn_down",
    )(act, w, x, gain.reshape(1, d))


def _tiles(m):
    if m >= 1024:
        return dict(inproj=1024, outproj=256, ffn_up=1024, ffn_down=512, norm=512)
    return dict(inproj=m, outproj=m, ffn_up=m, ffn_down=m, norm=m)


def _front_pad_rows(a, rows):
    return jnp.pad(a, ((0, 0), (rows - a.shape[1], 0), (0, 0)))


def _run_trunk(x, pos0, cache, states, wts, final_norm):
    batch, seq, d = x.shape
    m = batch * seq
    depth = wts["w_main"].shape[0]
    tiles = _tiles(m)
    x2 = x.reshape(m, d)
    h = _rmsnorm(x2, wts["norm1"][0], tiles["norm"], BF16)
    small = seq % SSD_CHUNK != 0
    mix_dtype = F32 if small else BF16
    outs = [[] for _ in range(6)]
    y_final = None
    for l in range(depth):
        proj, dt_raw = _inproj(h, wts["w_main"][l], wts["w_dt"][l], tiles["inproj"], SB_WIDTH)
        xbc = proj[:, COL_XBC:COL_XBC + SSD_CONV_DIM].reshape(batch, seq, SSD_CONV_DIM)
        xp = proj[:, COL_POOL:COL_POOL + POOL_WIDTH].reshape(batch, seq, POOL_WIDTH)
        outs[0].append(proj[:, COL_K:COL_K + SB_WIDTH].reshape(batch, seq, SB_HEADS, SB_HEAD_DIM))
        outs[1].append(proj[:, COL_V:COL_V + SB_WIDTH].reshape(batch, seq, SB_HEADS, SB_HEAD_DIM))
        if cache is None:
            ssm0 = jnp.zeros((batch, SSD_WIDTH, SSD_STATE), F32)
            conv_hist = jnp.zeros((batch, SSD_CONV - 1, SSD_CONV_DIM), F32)
            pool_hist = jnp.zeros((batch, POOL_HIST, POOL_WIDTH), F32)
            ffn_hist = jnp.zeros((batch, FFN_CONV - 1, wts["w_gate"].shape[2]), F32)
            o_sb = _attn_prompt(proj, wts["sb_bias"][l], batch, seq, 256)
        else:
            cache_k, cache_v, page_table = cache
            ssm0 = states[0][l].reshape(batch, SSD_WIDTH, SSD_STATE)
            conv_hist, pool_hist, ffn_hist = states[1][l], states[2][l], states[3][l]
            o_sb = _attn_sample(proj, wts["sb_bias"][l], cache_k, cache_v, page_table, l, batch, seq, 8)

        y_ssd, ssm_new = _ssd(proj, dt_raw, _front_pad_rows(conv_hist, SUBLANES), ssm0,
                              wts["ssd_conv_w"][l], wts["ssd_conv_b"][l], wts["dt_bias"][l], wts["a_log"][l],
                              wts["d_skip"][l], wts["ssd_norm"][l], batch, seq, mix_dtype)
        y_pool = _pool(proj, _front_pad_rows(pool_hist, 2 * SUBLANES), wts["pool_w"][l], wts["pool_scale"][l],
                       batch, seq, min(seq, 512), pos0, mix_dtype)
        outs[2].append(ssm_new.reshape(batch, SSD_HEADS, SSD_HEAD_DIM, SSD_STATE))
        outs[3].append(_tail_rows(conv_hist, xbc, SSD_CONV - 1))
        outs[4].append(_tail_rows(pool_hist, xp, POOL_HIST))

        x2, h2 = _outproj(o_sb, y_ssd, y_pool, x2, wts["w_out"][l], wts["norm2"][l], tiles["outproj"])
        act, tail = _ffn_up(h2, wts["w_gate"][l], wts["w_up"][l], _front_pad_rows(ffn_hist, SUBLANES),
                            wts["ffn_conv_w"][l], wts["ffn_conv_b"][l], batch, seq, tiles["ffn_up"], 512)
        dff = act.shape[1]
        if tiles["ffn_up"] <= seq:
            tps = seq // tiles["ffn_up"]
            g_last = tail.reshape(batch, tps, SUBLANES, dff)[:, tps - 1, SUBLANES - (FFN_CONV - 1):]
        else:
            g_last = tail.reshape(batch, seq, dff)[:, seq - (FFN_CONV - 1):]
        outs[5].append(g_last)
        last = l == depth - 1
        gain = final_norm if last else wts["norm1"][l + 1]
        x2, hn = _ffn_down(act, wts["w_down"][l], x2, gain, tiles["ffn_down"], 1408, F32 if last else BF16)
        if last:
            y_final = hn.reshape(batch, seq, d)
        else:
            h = hn
    return y_final, [jnp.stack(o) for o in outs]


def kernel(x_prompt, x_sample, cache_k, cache_v, page_table, state_ssm, state_ssm_conv, state_pool, state_ffn_conv, norm1, w_in, sb_bias, ssd_conv_w, ssd_conv_b, dt_bias, a_log, d_skip, ssd_norm, pool_w, pool_scale, w_out, norm2, w_gate, w_up, ffn_conv_w, ffn_conv_b, w_down, final_norm):
    c_q, c_z = 0, 3 * SB_WIDTH
    c_xbc = c_z + SSD_WIDTH
    c_dt = c_xbc + SSD_CONV_DIM
    c_pool = c_dt + SSD_HEADS
    w_main = jnp.concatenate([w_in[:, :, c_xbc:c_dt], w_in[:, :, c_q:c_z], w_in[:, :, c_z:c_xbc],
                              w_in[:, :, c_pool:]], axis=2).astype(BF16)
    w_dt = jnp.pad(w_in[:, :, c_dt:c_pool], ((0, 0), (0, 0), (0, LANES - SSD_HEADS))).astype(BF16)
    wts = dict(norm1=norm1, w_main=w_main, w_dt=w_dt, sb_bias=sb_bias, ssd_conv_w=ssd_conv_w,
               ssd_conv_b=ssd_conv_b, dt_bias=dt_bias, a_log=a_log, d_skip=d_skip, ssd_norm=ssd_norm,
               pool_w=pool_w.astype(BF16), pool_scale=pool_scale, w_out=w_out.astype(BF16), norm2=norm2,
               w_gate=w_gate.astype(BF16), w_up=w_up.astype(BF16), ffn_conv_w=ffn_conv_w,
               ffn_conv_b=ffn_conv_b, w_down=w_down.astype(BF16))

    y_p, st_p = _run_trunk(x_prompt, 0, None, None, wts, final_norm)

    depth, n_pool, page = cache_k.shape[:3]
    past_len = page_table.shape[1] * page
    cache = (cache_k.reshape(depth, n_pool, page, SB_WIDTH), cache_v.reshape(depth, n_pool, page, SB_WIDTH),
             page_table.reshape(-1))
    y_s, st_s = _run_trunk(x_sample, past_len, cache,
                           (state_ssm, state_ssm_conv, state_pool, state_ffn_conv), wts, final_norm)
    return (y_p, y_s, *st_p, *st_s)
```

```python
import functools
import math

import jax
import jax.numpy as jnp
from jax import lax
from jax.experimental import pallas as pl
from jax.experimental.pallas import tpu as pltpu

F32 = jnp.float32
BF16 = jnp.bfloat16

RMS_EPS = 1e-6
SB_HEADS = 8
SB_HEAD_DIM = 64
SB_WIDTH = SB_HEADS * SB_HEAD_DIM
SSD_HEADS = 16
SSD_HEAD_DIM = 64
SSD_WIDTH = SSD_HEADS * SSD_HEAD_DIM
SSD_GROUPS = 2
SSD_STATE = 128
SSD_CONV = 4
SSD_CHUNK = 128
SSD_CONV_DIM = SSD_WIDTH + 2 * SSD_GROUPS * SSD_STATE
POOL_WINDOWS = (2, 4, 8, 16)
POOL_GROUP_DIM = 128
POOL_WIDTH = POOL_GROUP_DIM * len(POOL_WINDOWS)
POOL_HIST = max(POOL_WINDOWS) - 1
FFN_CONV = 3

LANES = 128
SUBLANES = 8
VMEM_LIMIT = 56 * 1024 * 1024

COL_XBC = 0
COL_Q = SSD_CONV_DIM
COL_K = COL_Q + SB_WIDTH
COL_V = COL_K + SB_WIDTH
COL_Z = COL_V + SB_WIDTH
COL_POOL = COL_Z + SSD_WIDTH
PROJ_WIDTH = COL_POOL + POOL_WIDTH


def _cparams(sem):
    return pltpu.CompilerParams(dimension_semantics=sem, vmem_limit_bytes=VMEM_LIMIT)


def _softplus(x):
    return jnp.maximum(x, 0.0) + jnp.log1p(jnp.exp(-jnp.abs(x)))


def _silu(x):
    return x * jax.nn.sigmoid(x)


def _split_bf16(x, n):
    parts = []
    r = x
    for i in range(n):
        p = r.astype(BF16)
        parts.append(p)
        if i + 1 < n:
            r = r - p.astype(F32)
    return parts


def _dot_sel_rhs(x, sel, n):
    acc = None
    for p in _split_bf16(x, n):
        d = jnp.dot(p, sel, preferred_element_type=F32)
        acc = d if acc is None else acc + d
    return acc


def _dot_sel_lhs(sel, x, n):
    acc = None
    for p in _split_bf16(x, n):
        d = jnp.dot(sel, p, preferred_element_type=F32)
        acc = d if acc is None else acc + d
    return acc


def _dot_nt(a, b):
    return lax.dot_general(a, b, (((1,), (1,)), ((), ())), preferred_element_type=F32)


def _pad_rows(v, rows):
    if v.shape[0] == rows:
        return v
    return jnp.concatenate([v, jnp.zeros((rows - v.shape[0], v.shape[1]), v.dtype)], axis=0)


def _tail_rows(hist, x, n):
    seq = x.shape[1]
    if seq >= n:
        return x[:, seq - n:]
    return jnp.concatenate([hist[:, seq:], x], axis=1)


def _rms(x, gain):
    return x * lax.rsqrt(jnp.mean(x * x, axis=-1, keepdims=True) + RMS_EPS) * gain


def _norm_kernel(x_ref, g_ref, o_ref):
    o_ref[...] = _rms(x_ref[...], g_ref[...]).astype(o_ref.dtype)


def _rmsnorm(x, gain, tm, out_dtype):
    m, d = x.shape
    return pl.pallas_call(
        _norm_kernel,
        grid=(m // tm,),
        in_specs=[pl.BlockSpec((tm, d), lambda i: (i, 0)),
                  pl.BlockSpec((1, d), lambda i: (0, 0))],
        out_specs=pl.BlockSpec((tm, d), lambda i: (i, 0)),
        out_shape=jax.ShapeDtypeStruct((m, d), out_dtype),
        compiler_params=_cparams(("parallel",)),
        name="rmsnorm",
    )(x, gain.reshape(1, d))


def _inproj_kernel(h_ref, w_ref, wdt_ref, kall_ref, vall_ref, o_ref, dt_ref, knew_ref, vnew_ref):
    del kall_ref, vall_ref
    j = pl.program_id(1)
    h = h_ref[...]
    acc = jnp.dot(h, w_ref[...], preferred_element_type=F32)
    o_ref[...] = acc

    @pl.when(j == 0)
    def _():
        dt_ref[...] = jnp.dot(h, wdt_ref[...], preferred_element_type=F32)

    for col, new_ref in ((COL_K, knew_ref), (COL_V, vnew_ref)):
        @pl.when(j == col // SB_WIDTH)
        def _():
            for hd in range(SB_HEADS):
                new_ref[:, hd, :] = acc[:, SB_HEAD_DIM * hd:SB_HEAD_DIM * (hd + 1)]


def _inproj(h, w, wdt, k_all, v_all, layer, tm):
    m, d = h.shape
    n = w.shape[1]
    tn = SB_WIDTH
    new_spec = pl.BlockSpec((None, tm, SB_HEADS, SB_HEAD_DIM), lambda i, j: (layer, i, 0, 0))
    return pl.pallas_call(
        _inproj_kernel,
        grid=(m // tm, n // tn),
        in_specs=[pl.BlockSpec((tm, d), lambda i, j: (i, 0)),
                  pl.BlockSpec((d, tn), lambda i, j: (0, j)),
                  pl.BlockSpec((d, LANES), lambda i, j: (0, 0)),
                  pl.BlockSpec(memory_space=pl.ANY),
                  pl.BlockSpec(memory_space=pl.ANY)],
        out_specs=[pl.BlockSpec((tm, tn), lambda i, j: (i, j)),
                   pl.BlockSpec((tm, LANES), lambda i, j: (i, 0)),
                   new_spec, new_spec],
        out_shape=[jax.ShapeDtypeStruct((m, n), F32),
                   jax.ShapeDtypeStruct((m, LANES), F32),
                   jax.ShapeDtypeStruct(k_all.shape, F32),
                   jax.ShapeDtypeStruct(v_all.shape, F32)],
        input_output_aliases={3: 2, 4: 3},
        compiler_params=_cparams(("arbitrary", "arbitrary")),
        name="inproj",
    )(h, w, wdt, k_all, v_all)


def _tri_incl(n):
    return (lax.broadcasted_iota(jnp.int32, (n, n), 0) >= lax.broadcasted_iota(jnp.int32, (n, n), 1)).astype(BF16)


def _sb_tile(z, car, tri, mask):
    sp = jnp.maximum(z, 0.0) + jnp.log(1.0 + jnp.exp(-jnp.abs(z)))
    if mask is not None:
        sp = jnp.where(mask, sp, 0.0)
    cum = jnp.dot(sp.astype(BF16), tri, preferred_element_type=F32)
    w = jnp.exp(z - cum - car)
    if mask is not None:
        w = jnp.where(mask, w, 0.0)
    return w, jnp.sum(sp, axis=-1, keepdims=True)


def _attn_prompt_kernel(bias_ref, q_ref, k_ref, v_ref, tri_ref, o_ref, kb_ref, vb_ref, qh_ref, acc_ref, car_ref,
                        *, tq):
    i = pl.program_id(1)
    lane = lax.broadcasted_iota(jnp.int32, (1, LANES), 1)
    first_head = lane < SB_HEAD_DIM
    tri2 = tri_ref[...]
    n_pairs = SB_HEADS // 2

    @pl.when(i == 0)
    def _():
        kb_ref[...] = k_ref[...].astype(BF16)
        vb_ref[...] = v_ref[...].astype(BF16)

    for p in range(n_pairs):
        q2 = q_ref[:, LANES * p:LANES * (p + 1)] * (1.0 / math.sqrt(SB_HEAD_DIM))
        qh_ref[p, 0:tq, :] = jnp.where(first_head, q2, 0.0).astype(BF16)
        qh_ref[p, tq:2 * tq, :] = jnp.where(first_head, 0.0, q2).astype(BF16)
    acc_ref[...] = jnp.zeros_like(acc_ref)
    car_ref[...] = jnp.zeros_like(car_ref)

    def key_block(kb, mask):
        start = pl.multiple_of(kb * tq, tq)
        for p in range(n_pairs):
            cols = slice(LANES * p, LANES * (p + 1))
            s = _dot_nt(qh_ref[p], kb_ref[pl.ds(start, tq), cols])
            z = jnp.concatenate([s[0:tq] + bias_ref[2 * p], s[tq:2 * tq] + bias_ref[2 * p + 1]], axis=0)
            car = car_ref[p]
            w, ssum = _sb_tile(z, jnp.concatenate([car] * (tq // LANES), axis=1), tri2, mask)
            acc_ref[p] += jnp.dot(w.astype(BF16), vb_ref[pl.ds(start, tq), cols], preferred_element_type=F32)
            car_ref[p] = car + ssum

    row = lax.broadcasted_iota(jnp.int32, (2 * tq, tq), 0) & (tq - 1)
    col = lax.broadcasted_iota(jnp.int32, (2 * tq, tq), 1)
    key_block(i, col < row)

    def body(t, _):
        key_block(i - 1 - t, None)
        return 0

    lax.fori_loop(0, i, body, 0)
    for p in range(n_pairs):
        o_ref[:, LANES * p:LANES * (p + 1)] = jnp.where(
            first_head, acc_ref[p, 0:tq, :], acc_ref[p, tq:2 * tq, :]).astype(o_ref.dtype)


def _attn_prompt(proj, bias, batch, seq, tq):
    m = proj.shape[0]
    nq = seq // tq
    blk = SB_WIDTH
    return pl.pallas_call(
        functools.partial(_attn_prompt_kernel, tq=tq),
        grid=(batch, nq),
        in_specs=[pl.BlockSpec(memory_space=pltpu.SMEM),
                  pl.BlockSpec((tq, blk), lambda b, i: (b * nq + i, COL_Q // blk)),
                  pl.BlockSpec((seq, blk), lambda b, i: (b, COL_K // blk)),
                  pl.BlockSpec((seq, blk), lambda b, i: (b, COL_V // blk)),
                  pl.BlockSpec((tq, tq), lambda b, i: (0, 0))],
        out_specs=pl.BlockSpec((tq, blk), lambda b, i: (b * nq + i, 0)),
        out_shape=jax.ShapeDtypeStruct((m, blk), BF16),
        scratch_shapes=[pltpu.VMEM((seq, blk), BF16),
                        pltpu.VMEM((seq, blk), BF16),
                        pltpu.VMEM((SB_HEADS // 2, 2 * tq, LANES), BF16),
                        pltpu.VMEM((SB_HEADS // 2, 2 * tq, LANES), F32),
                        pltpu.VMEM((SB_HEADS // 2, 2 * tq, LANES), F32)],
        compiler_params=_cparams(("arbitrary", "arbitrary")),
        name="attn_prompt",
    )(bias, proj, proj, proj, _tri_incl(tq))


def _same_head(rows, n_new):
    rh = lax.broadcasted_iota(jnp.int32, (rows, SB_WIDTH), 0) >> (n_new.bit_length() - 1)
    ch = lax.broadcasted_iota(jnp.int32, (rows, SB_WIDTH), 1) >> (SB_HEAD_DIM.bit_length() - 1)
    return rh == ch


def _attn_sample_kernel(pt_ref, brow_ref, q_ref, kn_ref, vn_ref, tri_ref, fold_ref, spread_ref, gather_ref,
                        unfold_ref, *rest, pages_per_step, n_new):
    k_refs = rest[:pages_per_step]
    v_refs = rest[pages_per_step:2 * pages_per_step]
    o_ref = rest[2 * pages_per_step]
    qall_ref, acc_ref, opast_ref, car_ref = rest[2 * pages_per_step + 1:]
    s = pl.program_id(1)
    rows = SB_HEADS * n_new
    page = tri_ref.shape[1]
    tri = tri_ref[...]
    brow = brow_ref[...]

    @pl.when(s == 0)
    def _():
        q = q_ref[...] * (1.0 / math.sqrt(SB_HEAD_DIM))
        qt = jnp.concatenate([q] * SB_HEADS, axis=0)
        qbd = jnp.where(_same_head(rows, n_new), qt, 0.0).astype(BF16)
        qall_ref[...] = jnp.dot(qbd, fold_ref[...], preferred_element_type=F32).astype(BF16)
        opast_ref[...] = jnp.zeros_like(opast_ref)
        kn = _pad_rows(kn_ref[...], page).astype(BF16)
        vn = _pad_rows(vn_ref[...], page).astype(BF16)
        qi = lax.broadcasted_iota(jnp.int32, (rows, page), 0) & (n_new - 1)
        kj = lax.broadcasted_iota(jnp.int32, (rows, page), 1)
        w, ssum = _sb_tile(_dot_nt(qbd, kn) + brow, 0.0, tri, kj < qi)
        acc_ref[...] = jnp.dot(w.astype(BF16), vn, preferred_element_type=F32)
        car_ref[...] = jnp.broadcast_to(ssum, car_ref.shape)

    n_all = rows * pages_per_step
    r_idx = lax.broadcasted_iota(jnp.int32, (n_all, page * SB_HEADS), 0)
    c_idx = lax.broadcasted_iota(jnp.int32, (n_all, page * SB_HEADS), 1)
    head_match = ((r_idx >> (n_new.bit_length() - 1)) & (SB_HEADS - 1)) == (c_idx & (SB_HEADS - 1))
    qall = qall_ref[...]
    raw = jnp.concatenate([_dot_nt(qall, k_refs[j][...].astype(BF16)) for j in range(pages_per_step)], axis=0)
    raw = jnp.where(head_match, raw, 0.0)
    z = _dot_sel_rhs(raw, gather_ref[...], 2) + jnp.concatenate([brow] * pages_per_step, axis=0)

    sp = jnp.maximum(z, 0.0) + jnp.log(1.0 + jnp.exp(-jnp.abs(z)))
    cum = jnp.dot(sp.astype(BF16), tri, preferred_element_type=F32)
    ssum = jnp.sum(sp, axis=-1, keepdims=True)
    car = car_ref[...]
    cars = []
    for j in range(pages_per_step):
        cars.append(car)
        car = car + ssum[rows * j:rows * (j + 1)]
    car_ref[...] = car
    w = jnp.exp(z - cum - jnp.concatenate(cars, axis=0))
    wx = jnp.dot(w.astype(BF16), spread_ref[...], preferred_element_type=F32)
    wx = jnp.where(head_match, wx, 0.0).astype(BF16)
    o = opast_ref[...]
    for j in range(pages_per_step):
        o = o + jnp.dot(wx[rows * j:rows * (j + 1)], v_refs[j][...].astype(BF16), preferred_element_type=F32)
    opast_ref[...] = o

    @pl.when(s == pl.num_programs(1) - 1)
    def _():
        full = acc_ref[...] + _dot_sel_rhs(opast_ref[...], unfold_ref[...], 3)
        full = jnp.where(_same_head(rows, n_new), full, 0.0)
        out = full[0:n_new]
        for h in range(1, SB_HEADS):
            out = out + full[h * n_new:(h + 1) * n_new]
        o_ref[...] = out.astype(o_ref.dtype)


def _attn_sample(proj, bias, cache_k, cache_v, page_table, layer, batch, n_new, pages_per_step):
    page = cache_k.shape[2] // SB_HEADS
    n_pages = page_table.shape[0] // batch
    n_steps = n_pages // pages_per_step
    rows = SB_HEADS * n_new
    blk = SB_WIDTH
    tri = _tri_incl(page)
    brow = jnp.broadcast_to(jnp.repeat(bias, n_new)[:, None], (rows, page))
    iota = lambda shape, d: lax.broadcasted_iota(jnp.int32, shape, d)
    fold = (iota((blk, SB_HEAD_DIM), 0) % SB_HEAD_DIM == iota((blk, SB_HEAD_DIM), 1)).astype(BF16)
    gather = (iota((page * SB_HEADS, page), 0) // SB_HEADS == iota((page * SB_HEADS, page), 1)).astype(BF16)

    def page_map(j):
        def index_map(b, s, pt):
            return (layer, pt[b * n_pages + (n_pages - 1 - (s * pages_per_step + j))], 0, 0)
        return index_map

    page_specs = [pl.BlockSpec((None, None, page * SB_HEADS, SB_HEAD_DIM), page_map(j))
                  for j in range(pages_per_step)]
    const = lambda shape: pl.BlockSpec(shape, lambda b, s, pt: (0, 0))
    grid_spec = pltpu.PrefetchScalarGridSpec(
        num_scalar_prefetch=1,
        grid=(batch, n_steps),
        in_specs=[const((rows, page)),
                  pl.BlockSpec((n_new, blk), lambda b, s, pt: (b, COL_Q // blk)),
                  pl.BlockSpec((n_new, blk), lambda b, s, pt: (b, COL_K // blk)),
                  pl.BlockSpec((n_new, blk), lambda b, s, pt: (b, COL_V // blk)),
                  const((page, page)),
                  const(fold.shape), const(gather.T.shape), const(gather.shape), const(fold.T.shape)]
                 + page_specs + page_specs,
        out_specs=pl.BlockSpec((n_new, blk), lambda b, s, pt: (b, 0)),
        scratch_shapes=[pltpu.VMEM((rows, SB_HEAD_DIM), BF16),
                        pltpu.VMEM((rows, blk), F32),
                        pltpu.VMEM((rows, SB_HEAD_DIM), F32),
                        pltpu.VMEM((rows, page), F32)],
    )
    return pl.pallas_call(
        functools.partial(_attn_sample_kernel, pages_per_step=pages_per_step, n_new=n_new),
        grid_spec=grid_spec,
        out_shape=jax.ShapeDtypeStruct((batch * n_new, blk), F32),
        compiler_params=_cparams(("arbitrary", "arbitrary")),
        name="attn_sample",
    )(page_table, brow, proj, proj, proj, tri, fold, gather.T, gather, fold.T,
      *([cache_k] * pages_per_step), *([cache_v] * pages_per_step))


def _ssd_kernel(xbc_ref, z_ref, dt_ref, hist_ref, s0_ref, cw_ref, cb_ref, dtb_ref, alog_ref,
                dskip_ref, gain_ref, tri_ref, eexp_ref, y_ref, snew_ref, xh_scr, st_scr,
                *, rows_in):
    q = SSD_CHUNK
    c = pl.program_id(1)
    n_state = SSD_STATE
    gw = SSD_WIDTH // SSD_GROUPS

    @pl.when(c == 0)
    def _():
        xh_scr[0:SUBLANES, :] = hist_ref[...]
        for j in range(SSD_WIDTH // LANES):
            st_scr[:, LANES * j:LANES * (j + 1)] = s0_ref[LANES * j:LANES * (j + 1), :].T

    x = _pad_rows(xbc_ref[...], q)
    xh_scr[SUBLANES:SUBLANES + q, :] = x
    cw = cw_ref[...]
    conv = cb_ref[...] + x * cw[SSD_CONV - 1:SSD_CONV]
    for k in range(1, SSD_CONV):
        conv = conv + xh_scr[SUBLANES - k:SUBLANES - k + q, :] * cw[SSD_CONV - 1 - k:SSD_CONV - k]
    xh_scr[0:SUBLANES, :] = xh_scr[q:q + SUBLANES, :]
    xc = _silu(conv)
    xs = xc[:, :SSD_WIDTH]

    dt = _softplus(_pad_rows(dt_ref[...], q) + dtb_ref[...])
    if rows_in < q:
        rvalid = lax.broadcasted_iota(jnp.int32, (q, LANES), 0) < rows_in
        dt = jnp.where(rvalid, dt, 0.0)
    a = -jnp.exp(alog_ref[...])
    tri = tri_ref[...]
    cum = _dot_sel_lhs(tri, dt * a, 3)
    cum_t = cum.T
    dt_t = dt.T
    eexp = eexp_ref[...]
    cum_x = _dot_sel_rhs(cum, eexp, 3)
    dt_x = _dot_sel_rhs(dt, eexp, 3)
    cend_x = cum_x[q - 1:q, :]
    to_end_x = jnp.exp(cend_x - cum_x) * dt_x
    xw = (xs * to_end_x).astype(BF16)

    tt = lax.broadcasted_iota(jnp.int32, (q, q), 0)
    ss = lax.broadcasted_iota(jnp.int32, (q, q), 1)
    causal = ss <= tt
    lane = lax.broadcasted_iota(jnp.int32, (1, LANES), 1)
    first_head = lane < SSD_HEAD_DIM

    pairs_per_group = gw // LANES
    y_groups = []
    for g in range(SSD_GROUPS):
        bg = xc[:, SSD_WIDTH + g * n_state:SSD_WIDTH + (g + 1) * n_state]
        cg = xc[:, SSD_WIDTH + (SSD_GROUPS + g) * n_state:SSD_WIDTH + (SSD_GROUPS + g + 1) * n_state]
        cg16 = cg.astype(BF16)
        gcols = slice(gw * g, gw * (g + 1))
        y_off = jnp.dot(cg16, st_scr[:, gcols].astype(BF16), preferred_element_type=F32)
        cb = _dot_nt(cg16, bg.astype(BF16))
        y_pairs = []
        for jj in range(pairs_per_group):
            j = g * pairs_per_group + jj
            xs2 = xs[:, LANES * j:LANES * (j + 1)].astype(BF16)
            halves = []
            for hl in range(2):
                h = 2 * j + hl
                seg = cum[:, h:h + 1] - cum_t[h:h + 1, :]
                decay = jnp.exp(jnp.where(causal, seg, -1e30))
                wts = cb * decay * dt_t[h:h + 1, :]
                halves.append(jnp.dot(wts.astype(BF16), xs2, preferred_element_type=F32))
            y_pairs.append(jnp.where(first_head, halves[0], halves[1]))
        y_groups.append(jnp.concatenate(y_pairs, axis=1) + y_off * jnp.exp(cum_x[:, gcols]))
        st_scr[:, gcols] = (st_scr[:, gcols] * jnp.exp(cend_x[:, gcols])
                            + jnp.dot(bg.T.astype(BF16), xw[:, gcols], preferred_element_type=F32))

    y = jnp.concatenate(y_groups, axis=1) + dskip_ref[...] * xs
    y = _rms(y * _silu(_pad_rows(z_ref[...], q)), gain_ref[...])
    y_ref[...] = y[0:rows_in].astype(y_ref.dtype)

    @pl.when(c == pl.num_programs(1) - 1)
    def _():
        for j in range(SSD_WIDTH // LANES):
            snew_ref[LANES * j:LANES * (j + 1), :] = st_scr[:, LANES * j:LANES * (j + 1)].T


def _ssd(proj, dt_raw, hist, s0, conv_w, conv_b, dt_bias, a_log, d_skip, gain, batch, seq, out_dtype):
    q = SSD_CHUNK
    m = proj.shape[0]
    rows_in = q if seq % q == 0 else seq
    nc = seq // rows_in
    tri = (lax.broadcasted_iota(jnp.int32, (q, q), 1)
           <= lax.broadcasted_iota(jnp.int32, (q, q), 0)).astype(BF16)
    eexp = (lax.broadcasted_iota(jnp.int32, (LANES, SSD_WIDTH), 0)
            == lax.broadcasted_iota(jnp.int32, (LANES, SSD_WIDTH), 1) // SSD_HEAD_DIM).astype(BF16)
    pad_h = LANES - SSD_HEADS
    cw = jnp.pad(conv_w, ((0, SUBLANES - SSD_CONV), (0, 0)))
    const = lambda shape: pl.BlockSpec(shape, lambda b, c: (0,) * len(shape))
    return pl.pallas_call(
        functools.partial(_ssd_kernel, rows_in=rows_in),
        grid=(batch, nc),
        in_specs=[pl.BlockSpec((rows_in, SSD_CONV_DIM), lambda b, c: (b * nc + c, COL_XBC // SSD_CONV_DIM)),
                  pl.BlockSpec((rows_in, SSD_WIDTH), lambda b, c: (b * nc + c, COL_Z // SSD_WIDTH)),
                  pl.BlockSpec((rows_in, LANES), lambda b, c: (b * nc + c, 0)),
                  pl.BlockSpec((None, SUBLANES, SSD_CONV_DIM), lambda b, c: (b, 0, 0)),
                  pl.BlockSpec((None, SSD_WIDTH, SSD_STATE), lambda b, c: (b, 0, 0)),
                  const((SUBLANES, SSD_CONV_DIM)),
                  const((1, SSD_CONV_DIM)),
                  const((1, LANES)),
                  const((1, LANES)),
                  const((1, SSD_WIDTH)),
                  const((1, SSD_WIDTH)),
                  const((q, q)),
                  const((LANES, SSD_WIDTH))],
        out_specs=[pl.BlockSpec((rows_in, SSD_WIDTH), lambda b, c: (b * nc + c, 0)),
                   pl.BlockSpec((None, SSD_WIDTH, SSD_STATE), lambda b, c: (b, 0, 0))],
        out_shape=[jax.ShapeDtypeStruct((m, SSD_WIDTH), out_dtype),
                   jax.ShapeDtypeStruct((batch, SSD_WIDTH, SSD_STATE), F32)],
        scratch_shapes=[pltpu.VMEM((SUBLANES + q + SUBLANES, SSD_CONV_DIM), F32),
                        pltpu.VMEM((SSD_STATE, SSD_WIDTH), F32)],
        compiler_params=_cparams(("parallel", "arbitrary")),
        name="ssd",
    )(proj, proj, dt_raw, hist, s0, cw, conv_b.reshape(1, -1),
      jnp.pad(dt_bias, (0, pad_h)).reshape(1, LANES), jnp.pad(a_log, (0, pad_h)).reshape(1, LANES),
      jnp.repeat(d_skip, SSD_HEAD_DIM).reshape(1, SSD_WIDTH), gain.reshape(1, SSD_WIDTH), tri, eexp)


def _pool_kernel(x_ref, hist_ref, w_ref, scale_ref, y_ref, buf, *, tm, pos0, carry):
    hist_rows = 2 * SUBLANES
    t = pl.program_id(1)

    @pl.when(t == 0)
    def _():
        buf[0:hist_rows, :] = hist_ref[...]

    x = x_ref[...]
    buf[hist_rows:hist_rows + tm, :] = x
    pos = pos0 + t * tm + lax.broadcasted_iota(jnp.int32, (tm, 1), 0)
    for g, win in enumerate(POOL_WINDOWS):
        cols = slice(POOL_GROUP_DIM * g, POOL_GROUP_DIM * (g + 1))
        xg = x[:, cols]
        wsum = xg
        for k in range(1, win):
            wsum = wsum + buf[hist_rows - k:hist_rows - k + tm, cols]
        count = jnp.minimum(win, pos + 1).astype(F32)
        diff = wsum / count - xg
        yg = jnp.dot(diff.astype(BF16), w_ref[g], preferred_element_type=F32) * scale_ref[:, cols]
        y_ref[:, cols] = yg.astype(y_ref.dtype)
    if carry:
        buf[0:hist_rows, :] = buf[tm:tm + hist_rows, :]


def _pool(proj, hist, w, scale, batch, seq, tm, pos0, out_dtype):
    m = proj.shape[0]
    nt = seq // tm
    blk = POOL_WIDTH
    return pl.pallas_call(
        functools.partial(_pool_kernel, tm=tm, pos0=pos0, carry=nt > 1),
        grid=(batch, nt),
        in_specs=[pl.BlockSpec((tm, blk), lambda b, t: (b * nt + t, COL_POOL // blk)),
                  pl.BlockSpec((None, 2 * SUBLANES, blk), lambda b, t: (b, 0, 0)),
                  pl.BlockSpec((len(POOL_WINDOWS), POOL_GROUP_DIM, POOL_GROUP_DIM), lambda b, t: (0, 0, 0)),
                  pl.BlockSpec((1, blk), lambda b, t: (0, 0))],
        out_specs=pl.BlockSpec((tm, blk), lambda b, t: (b * nt + t, 0)),
        out_shape=jax.ShapeDtypeStruct((m, blk), out_dtype),
        scratch_shapes=[pltpu.VMEM((2 * SUBLANES + tm, blk), F32)],
        compiler_params=_cparams(("parallel", "arbitrary")),
        name="pool",
    )(proj, hist, w, scale.reshape(1, blk))


def _outproj_kernel(o_ref, y_ref, p_ref, x_ref, w_ref, g_ref, xn_ref, h_ref):
    a, b = SB_WIDTH, SB_WIDTH + SSD_WIDTH
    acc = x_ref[...]
    acc = acc + jnp.dot(o_ref[...].astype(BF16), w_ref[0:a, :], preferred_element_type=F32)
    acc = acc + jnp.dot(y_ref[...].astype(BF16), w_ref[a:b, :], preferred_element_type=F32)
    acc = acc + jnp.dot(p_ref[...].astype(BF16), w_ref[b:, :], preferred_element_type=F32)
    xn_ref[...] = acc
    h_ref[...] = _rms(acc, g_ref[...]).astype(h_ref.dtype)


def _outproj(o, y, p, x, w, gain, tm):
    m, d = x.shape
    row = lambda width: pl.BlockSpec((tm, width), lambda i: (i, 0))
    return pl.pallas_call(
        _outproj_kernel,
        grid=(m // tm,),
        in_specs=[row(SB_WIDTH), row(SSD_WIDTH), row(POOL_WIDTH), row(d),
                  pl.BlockSpec(w.shape, lambda i: (0, 0)),
                  pl.BlockSpec((1, d), lambda i: (0, 0))],
        out_specs=[row(d), row(d)],
        out_shape=[jax.ShapeDtypeStruct((m, d), F32), jax.ShapeDtypeStruct((m, d), BF16)],
        compiler_params=_cparams(("parallel",)),
        name="outproj",
    )(o, y, p, x, w, gain.reshape(1, d))


def _ffn_up_kernel(h_ref, wg_ref, wu_ref, hist_ref, cw_ref, cb_ref, act_ref, tail_ref, buf, wg16, wu16,
                   *, seg_rows, n_seg, tiles_per_seq):
    @pl.when(pl.program_id(1) == 0)
    def _():
        wg16[...] = wg_ref[...].astype(BF16)
        wu16[...] = wu_ref[...].astype(BF16)

    h = h_ref[...]
    g = jnp.dot(h, wg16[...], preferred_element_type=F32)
    u = jnp.dot(h, wu16[...], preferred_element_type=F32)
    cw = cw_ref[...]
    cb = cb_ref[...]

    def conv_rows(gs):
        buf[SUBLANES:SUBLANES + seg_rows, :] = gs
        out = cb + gs * cw[FFN_CONV - 1:FFN_CONV]
        for k in range(1, FFN_CONV):
            out = out + buf[SUBLANES - k:SUBLANES - k + seg_rows, :] * cw[FFN_CONV - 1 - k:FFN_CONV - k]
        return out

    if n_seg == 1:
        @pl.when(pl.program_id(1) % tiles_per_seq == 0)
        def _():
            buf[0:SUBLANES, :] = hist_ref[...]
        gc = conv_rows(g)
        buf[0:SUBLANES, :] = buf[seg_rows:seg_rows + SUBLANES, :]
        tail_ref[...] = g[seg_rows - SUBLANES:seg_rows]
        act_ref[...] = (_silu(gc) * u).astype(act_ref.dtype)
    else:
        for s in range(n_seg):
            rows = slice(seg_rows * s, seg_rows * (s + 1))
            buf[0:SUBLANES, :] = hist_ref[s]
            gc = conv_rows(g[rows])
            act_ref[rows, :] = (_silu(gc) * u[rows]).astype(act_ref.dtype)
        tail_ref[...] = g


def _ffn_up(h, wg, wu, hist, conv_w, conv_b, batch, seq, tm, tn):
    m, d = h.shape
    dff = wg.shape[1]
    nt = m // tm
    cw = jnp.pad(conv_w, ((0, SUBLANES - FFN_CONV), (0, 0)))
    if tm <= seq:
        n_seg, seg_rows, tps = 1, tm, seq // tm
        hist_spec = pl.BlockSpec((None, SUBLANES, tn), lambda j, i: (i // tps, 0, j))
        tail_spec = pl.BlockSpec((None, SUBLANES, tn), lambda j, i: (i, 0, j))
        tail_shape = jax.ShapeDtypeStruct((nt, SUBLANES, dff), F32)
    else:
        n_seg, seg_rows, tps = tm // seq, seq, 1
        hist_spec = pl.BlockSpec((n_seg, SUBLANES, tn), lambda j, i: (i, 0, j))
        tail_spec = pl.BlockSpec((tm, tn), lambda j, i: (i, j))
        tail_shape = jax.ShapeDtypeStruct((m, dff), F32)
    return pl.pallas_call(
        functools.partial(_ffn_up_kernel, seg_rows=seg_rows, n_seg=n_seg, tiles_per_seq=tps),
        grid=(dff // tn, nt),
        in_specs=[pl.BlockSpec((tm, d), lambda j, i: (i, 0)),
                  pl.BlockSpec((d, tn), lambda j, i: (0, j)),
                  pl.BlockSpec((d, tn), lambda j, i: (0, j)),
                  hist_spec,
                  pl.BlockSpec((SUBLANES, tn), lambda j, i: (0, j)),
                  pl.BlockSpec((1, tn), lambda j, i: (0, j))],
        out_specs=[pl.BlockSpec((tm, tn), lambda j, i: (i, j)), tail_spec],
        out_shape=[jax.ShapeDtypeStruct((m, dff), BF16), tail_shape],
        scratch_shapes=[pltpu.VMEM((SUBLANES + seg_rows + SUBLANES, tn), F32),
                        pltpu.VMEM((d, tn), BF16),
                        pltpu.VMEM((d, tn), BF16)],
        compiler_params=_cparams(("arbitrary", "arbitrary")),
        name="ffn_up",
    )(h, wg, wu, hist, cw, conv_b.reshape(1, dff))


def _ffn_down_kernel(a_ref, w_ref, x_ref, g_ref, xn_ref, h_ref, *, tn):
    j = pl.program_id(1)
    part = jnp.dot(a_ref[...], w_ref[...], preferred_element_type=F32)
    for c in range(x_ref.shape[1] // tn):
        @pl.when(j == c)
        def _():
            xn_ref[:, tn * c:tn * (c + 1)] = x_ref[:, tn * c:tn * (c + 1)] + part

    @pl.when(j == pl.num_programs(1) - 1)
    def _():
        h_ref[...] = _rms(xn_ref[...], g_ref[...]).astype(h_ref.dtype)


def _ffn_down(act, w, x, gain, tm, tn, norm_dtype):
    m, d = x.shape
    dff = act.shape[1]
    return pl.pallas_call(
        functools.partial(_ffn_down_kernel, tn=tn),
        grid=(m // tm, d // tn),
        in_specs=[pl.BlockSpec((tm, dff), lambda i, j: (i, 0)),
                  pl.BlockSpec((dff, tn), lambda i, j: (0, j)),
                  pl.BlockSpec((tm, d), lambda i, j: (i, 0)),
                  pl.BlockSpec((1, d), lambda i, j: (0, 0))],
        out_specs=[pl.BlockSpec((tm, d), lambda i, j: (i, 0)),
                   pl.BlockSpec((tm, d), lambda i, j: (i, 0))],
        out_shape=[jax.ShapeDtypeStruct((m, d), F32), jax.ShapeDtypeStruct((m, d), norm_dtype)],
        compiler_params=_cparams(("arbitrary", "arbitrary")),
        name="ffn_down",
    )(act, w, x, gain.reshape(1, d))


def _tiles(m):
    if m >= 1024:
        return dict(inproj=1024, outproj=256, ffn_up=1024, ffn_down=512, norm=512)
    return dict(inproj=m, outproj=m, ffn_up=m, ffn_down=m, norm=m)


def _front_pad_rows(a, rows):
    return jnp.pad(a, ((0, 0), (rows - a.shape[1], 0), (0, 0)))


def _run_trunk(x, pos0, cache, states, wts, final_norm):
    batch, seq, d = x.shape
    m = batch * seq
    depth = wts["w_main"].shape[0]
    tiles = _tiles(m)
    x2 = x.reshape(m, d)
    h = _rmsnorm(x2, wts["norm1"][0], tiles["norm"], BF16)
    small = seq % SSD_CHUNK != 0
    mix_dtype = F32 if small else BF16
    outs = [[] for _ in range(4)]
    k_all = jnp.zeros((depth, m, SB_HEADS, SB_HEAD_DIM), F32)
    v_all = jnp.zeros((depth, m, SB_HEADS, SB_HEAD_DIM), F32)
    y_final = None
    for l in range(depth):
        proj, dt_raw, k_all, v_all = _inproj(h, wts["w_main"][l], wts["w_dt"][l], k_all, v_all, l, tiles["inproj"])
        last_rows = proj.reshape(batch, seq, PROJ_WIDTH)[:, max(0, seq - POOL_HIST):]
        xbc = last_rows[:, :, COL_XBC:COL_XBC + SSD_CONV_DIM]
        xp = last_rows[:, :, COL_POOL:COL_POOL + POOL_WIDTH]
        if cache is None:
            ssm0 = jnp.zeros((batch, SSD_WIDTH, SSD_STATE), F32)
            conv_hist = jnp.zeros((batch, SSD_CONV - 1, SSD_CONV_DIM), F32)
            pool_hist = jnp.zeros((batch, POOL_HIST, POOL_WIDTH), F32)
            ffn_hist = jnp.zeros((batch, FFN_CONV - 1, wts["w_gate"].shape[2]), F32)
            o_sb = _attn_prompt(proj, wts["sb_bias"][l], batch, seq, 256)
        else:
            cache_k, cache_v, page_table = cache
            ssm0 = states[0][l].reshape(batch, SSD_WIDTH, SSD_STATE)
            conv_hist, pool_hist, ffn_hist = states[1][l], states[2][l], states[3][l]
            o_sb = _attn_sample(proj, wts["sb_bias"][l], cache_k, cache_v, page_table, l, batch, seq, 8)

        y_ssd, ssm_new = _ssd(proj, dt_raw, _front_pad_rows(conv_hist, SUBLANES), ssm0,
                              wts["ssd_conv_w"][l], wts["ssd_conv_b"][l], wts["dt_bias"][l], wts["a_log"][l],
                              wts["d_skip"][l], wts["ssd_norm"][l], batch, seq, mix_dtype)
        y_pool = _pool(proj, _front_pad_rows(pool_hist, 2 * SUBLANES), wts["pool_w"][l], wts["pool_scale"][l],
                       batch, seq, min(seq, 512), pos0, mix_dtype)
        outs[0].append(ssm_new.reshape(batch, SSD_HEADS, SSD_HEAD_DIM, SSD_STATE))
        outs[1].append(_tail_rows(conv_hist, xbc, SSD_CONV - 1))
        outs[2].append(_tail_rows(pool_hist, xp, POOL_HIST))

        x2, h2 = _outproj(o_sb, y_ssd, y_pool, x2, wts["w_out"][l], wts["norm2"][l], tiles["outproj"])
        act, tail = _ffn_up(h2, wts["w_gate"][l], wts["w_up"][l], _front_pad_rows(ffn_hist, SUBLANES),
                            wts["ffn_conv_w"][l], wts["ffn_conv_b"][l], batch, seq, tiles["ffn_up"], 512)
        dff = act.shape[1]
        if tiles["ffn_up"] <= seq:
            tps = seq // tiles["ffn_up"]
            g_last = tail.reshape(batch, tps, SUBLANES, dff)[:, tps - 1, SUBLANES - (FFN_CONV - 1):]
        else:
            g_last = tail.reshape(batch, seq, dff)[:, seq - (FFN_CONV - 1):]
        outs[3].append(g_last)
        last = l == depth - 1
        gain = final_norm if last else wts["norm1"][l + 1]
        x2, hn = _ffn_down(act, wts["w_down"][l], x2, gain, tiles["ffn_down"], 512, F32 if last else BF16)
        if last:
            y_final = hn.reshape(batch, seq, d)
        else:
            h = hn
    kv_shape = (depth, batch, seq, SB_HEADS, SB_HEAD_DIM)
    return y_final, [k_all.reshape(kv_shape), v_all.reshape(kv_shape)] + [jnp.stack(o) for o in outs]


def kernel(x_prompt, x_sample, cache_k, cache_v, page_table, state_ssm, state_ssm_conv, state_pool, state_ffn_conv, norm1, w_in, sb_bias, ssd_conv_w, ssd_conv_b, dt_bias, a_log, d_skip, ssd_norm, pool_w, pool_scale, w_out, norm2, w_gate, w_up, ffn_conv_w, ffn_conv_b, w_down, final_norm):
    c_q, c_z = 0, 3 * SB_WIDTH
    c_xbc = c_z + SSD_WIDTH
    c_dt = c_xbc + SSD_CONV_DIM
    c_pool = c_dt + SSD_HEADS
    w_main = jnp.concatenate([w_in[:, :, c_xbc:c_dt], w_in[:, :, c_q:c_z], w_in[:, :, c_z:c_xbc],
                              w_in[:, :, c_pool:]], axis=2).astype(BF16)
    w_dt = jnp.pad(w_in[:, :, c_dt:c_pool], ((0, 0), (0, 0), (0, LANES - SSD_HEADS))).astype(BF16)
    wts = dict(norm1=norm1, w_main=w_main, w_dt=w_dt, sb_bias=sb_bias, ssd_conv_w=ssd_conv_w,
               ssd_conv_b=ssd_conv_b, dt_bias=dt_bias, a_log=a_log, d_skip=d_skip, ssd_norm=ssd_norm,
               pool_w=pool_w.astype(BF16), pool_scale=pool_scale, w_out=w_out.astype(BF16), norm2=norm2,
               w_gate=w_gate, w_up=w_up, ffn_conv_w=ffn_conv_w,
               ffn_conv_b=ffn_conv_b, w_down=w_down.astype(BF16))

    y_p, st_p = _run_trunk(x_prompt, 0, None, None, wts, final_norm)

    depth, n_pool, page = cache_k.shape[:3]
    past_len = page_table.shape[1] * page
    cache = (cache_k.reshape(depth, n_pool, page * SB_HEADS, SB_HEAD_DIM),
             cache_v.reshape(depth, n_pool, page * SB_HEADS, SB_HEAD_DIM), page_table.reshape(-1))
    y_s, st_s = _run_trunk(x_sample, past_len, cache,
                           (state_ssm, state_ssm_conv, state_pool, state_ffn_conv), wts, final_norm)
    return (y_p, y_s, *st_p, *st_s)
```

```python
import functools
import math

import jax
import jax.numpy as jnp
from jax import lax
from jax.experimental import pallas as pl
from jax.experimental.pallas import tpu as pltpu

F32 = jnp.float32
BF16 = jnp.bfloat16

RMS_EPS = 1e-6
SB_HEADS = 8
SB_HEAD_DIM = 64
SB_WIDTH = SB_HEADS * SB_HEAD_DIM
SSD_HEADS = 16
SSD_HEAD_DIM = 64
SSD_WIDTH = SSD_HEADS * SSD_HEAD_DIM
SSD_GROUPS = 2
SSD_STATE = 128
SSD_CONV = 4
SSD_CHUNK = 128
SSD_CONV_DIM = SSD_WIDTH + 2 * SSD_GROUPS * SSD_STATE
POOL_WINDOWS = (2, 4, 8, 16)
POOL_GROUP_DIM = 128
POOL_WIDTH = POOL_GROUP_DIM * len(POOL_WINDOWS)
POOL_HIST = max(POOL_WINDOWS) - 1
FFN_CONV = 3

LANES = 128
SUBLANES = 8
VMEM_LIMIT = 56 * 1024 * 1024

COL_XBC = 0
COL_Q = SSD_CONV_DIM
COL_Z = COL_Q + SB_WIDTH
COL_POOL = COL_Z + SSD_WIDTH
PROJ_WIDTH = COL_POOL + POOL_WIDTH


def _cparams(sem):
    return pltpu.CompilerParams(dimension_semantics=sem, vmem_limit_bytes=VMEM_LIMIT)


def _softplus(x):
    return jnp.maximum(x, 0.0) + jnp.log1p(jnp.exp(-jnp.abs(x)))


def _silu(x):
    return x * jax.nn.sigmoid(x)


def _split_bf16(x, n):
    parts = []
    r = x
    for i in range(n):
        p = r.astype(BF16)
        parts.append(p)
        if i + 1 < n:
            r = r - p.astype(F32)
    return parts


def _dot_sel_rhs(x, sel, n):
    acc = None
    for p in _split_bf16(x, n):
        d = jnp.dot(p, sel, preferred_element_type=F32)
        acc = d if acc is None else acc + d
    return acc


def _dot_sel_lhs(sel, x, n):
    acc = None
    for p in _split_bf16(x, n):
        d = jnp.dot(sel, p, preferred_element_type=F32)
        acc = d if acc is None else acc + d
    return acc


def _dot_nt(a, b):
    return lax.dot_general(a, b, (((1,), (1,)), ((), ())), preferred_element_type=F32)


def _pad_rows(v, rows):
    if v.shape[0] == rows:
        return v
    return jnp.concatenate([v, jnp.zeros((rows - v.shape[0], v.shape[1]), v.dtype)], axis=0)


def _tail_rows(hist, x, n):
    seq = x.shape[1]
    if seq >= n:
        return x[:, seq - n:]
    return jnp.concatenate([hist[:, seq:], x], axis=1)


def _rms(x, gain):
    return x * lax.rsqrt(jnp.mean(x * x, axis=-1, keepdims=True) + RMS_EPS) * gain


def _norm_kernel(x_ref, g_ref, o_ref):
    o_ref[...] = _rms(x_ref[...], g_ref[...]).astype(o_ref.dtype)


def _rmsnorm(x, gain, tm, out_dtype):
    m, d = x.shape
    return pl.pallas_call(
        _norm_kernel,
        grid=(m // tm,),
        in_specs=[pl.BlockSpec((tm, d), lambda i: (i, 0)),
                  pl.BlockSpec((1, d), lambda i: (0, 0))],
        out_specs=pl.BlockSpec((tm, d), lambda i: (i, 0)),
        out_shape=jax.ShapeDtypeStruct((m, d), out_dtype),
        compiler_params=_cparams(("parallel",)),
        name="rmsnorm",
    )(x, gain.reshape(1, d))


def _inproj_kernel(h_ref, w_ref, wdt_ref, wk_ref, wv_ref, kall_ref, vall_ref, o_ref, dt_ref, *kv_refs,
                   kv_transposed):
    del kall_ref, vall_ref
    h = h_ref[...]
    o_ref[...] = _dot_nt(h, w_ref[...])

    @pl.when(pl.program_id(1) == 0)
    def _():
        dt_ref[...] = _dot_nt(h, wdt_ref[...])
        if kv_transposed:
            kt_ref, vt_ref = kv_refs
            kt_ref[...] = _dot_nt(wk_ref[...], h)
            vt_ref[...] = _dot_nt(wv_ref[...], h)
        else:
            kn_ref, vn_ref, k4_ref, v4_ref = kv_refs
            for w_kv, flat_ref, split_ref in ((wk_ref, kn_ref, k4_ref), (wv_ref, vn_ref, v4_ref)):
                new = _dot_nt(h, w_kv[...])
                flat_ref[...] = new
                for hd in range(SB_HEADS):
                    split_ref[:, hd, :] = new[:, SB_HEAD_DIM * hd:SB_HEAD_DIM * (hd + 1)]


def _inproj(h, w, wdt, wk, wv, k_all, v_all, layer, tm, seq, kv_transposed):
    m, d = h.shape
    n = w.shape[1]
    tn = SB_WIDTH
    wspec = lambda rows, imap: pl.BlockSpec((None, rows, d), imap)
    flat = jax.ShapeDtypeStruct((m, SB_WIDTH), F32)
    if kv_transposed:
        tps = seq // tm
        new_spec = pl.BlockSpec((None, None, SB_WIDTH, tm), lambda i, j: (layer, i // tps, 0, i % tps))
        kv_specs, kv_shapes = [new_spec, new_spec], []
    else:
        new_spec = pl.BlockSpec((None, tm, SB_HEADS, SB_HEAD_DIM), lambda i, j: (layer, i, 0, 0))
        flat_spec = pl.BlockSpec((tm, SB_WIDTH), lambda i, j: (i, 0))
        kv_specs, kv_shapes = [flat_spec, flat_spec, new_spec, new_spec], [flat, flat]
    n_out = 2 + len(kv_specs)
    return pl.pallas_call(
        functools.partial(_inproj_kernel, kv_transposed=kv_transposed),
        grid=(m // tm, n // tn),
        in_specs=[pl.BlockSpec((tm, d), lambda i, j: (i, 0)),
                  wspec(tn, lambda i, j: (layer, j, 0)),
                  wspec(LANES, lambda i, j: (layer, 0, 0)),
                  wspec(SB_WIDTH, lambda i, j: (layer, 0, 0)),
                  wspec(SB_WIDTH, lambda i, j: (layer, 0, 0)),
                  pl.BlockSpec(memory_space=pl.ANY),
                  pl.BlockSpec(memory_space=pl.ANY)],
        out_specs=[pl.BlockSpec((tm, tn), lambda i, j: (i, j)),
                   pl.BlockSpec((tm, LANES), lambda i, j: (i, 0))] + kv_specs,
        out_shape=[jax.ShapeDtypeStruct((m, n), F32),
                   jax.ShapeDtypeStruct((m, LANES), F32)] + kv_shapes
                  + [jax.ShapeDtypeStruct(k_all.shape, F32), jax.ShapeDtypeStruct(v_all.shape, F32)],
        input_output_aliases={5: n_out - 2, 6: n_out - 1},
        compiler_params=_cparams(("arbitrary", "arbitrary")),
        name="inproj",
    )(h, w, wdt, wk, wv, k_all, v_all)


def _tri_incl(n):
    return (lax.broadcasted_iota(jnp.int32, (n, n), 0) >= lax.broadcasted_iota(jnp.int32, (n, n), 1)).astype(BF16)


def _sb_tile(z, car, tri, mask):
    sp = jnp.maximum(z, 0.0) + jnp.log(1.0 + jnp.exp(-jnp.abs(z)))
    if mask is not None:
        sp = jnp.where(mask, sp, 0.0)
    cum = jnp.dot(sp.astype(BF16), tri, preferred_element_type=F32)
    w = jnp.exp(z - cum - car)
    if mask is not None:
        w = jnp.where(mask, w, 0.0)
    return w, jnp.sum(sp, axis=-1, keepdims=True)


def _attn_prompt_kernel(bias_ref, q_ref, kt_ref, vt_ref, tri_ref, o_ref, kb_ref, vb_ref, qh_ref, acc_ref, car_ref,
                        *, tq):
    i = pl.program_id(1)
    lane = lax.broadcasted_iota(jnp.int32, (1, LANES), 1)
    first_head = lane < SB_HEAD_DIM
    tri2 = tri_ref[...]
    n_pairs = SB_HEADS // 2

    @pl.when(i == 0)
    def _():
        for c in range(kt_ref.shape[1] // tq):
            kb_ref[c] = kt_ref[:, tq * c:tq * (c + 1)].astype(BF16)
            vb_ref[c] = vt_ref[:, tq * c:tq * (c + 1)].astype(BF16)

    for p in range(n_pairs):
        q2 = q_ref[:, LANES * p:LANES * (p + 1)] * (1.0 / math.sqrt(SB_HEAD_DIM))
        qh_ref[p, 0:tq, :] = jnp.where(first_head, q2, 0.0).astype(BF16)
        qh_ref[p, tq:2 * tq, :] = jnp.where(first_head, 0.0, q2).astype(BF16)
    acc_ref[...] = jnp.zeros_like(acc_ref)
    car_ref[...] = jnp.zeros_like(car_ref)

    def key_block(kb, mask):
        for p in range(n_pairs):
            pair = slice(LANES * p, LANES * (p + 1))
            s = jnp.dot(qh_ref[p], kb_ref[kb, pair, :], preferred_element_type=F32)
            z = jnp.concatenate([s[0:tq] + bias_ref[2 * p], s[tq:2 * tq] + bias_ref[2 * p + 1]], axis=0)
            car = car_ref[p]
            w, ssum = _sb_tile(z, jnp.concatenate([car] * (tq // LANES), axis=1), tri2, mask)
            acc_ref[p] += _dot_nt(w.astype(BF16), vb_ref[kb, pair, :])
            car_ref[p] = car + ssum

    row = lax.broadcasted_iota(jnp.int32, (2 * tq, tq), 0) & (tq - 1)
    col = lax.broadcasted_iota(jnp.int32, (2 * tq, tq), 1)
    key_block(i, col < row)

    def body(t, _):
        key_block(i - 1 - t, None)
        return 0

    lax.fori_loop(0, i, body, 0)
    for p in range(n_pairs):
        o_ref[:, LANES * p:LANES * (p + 1)] = jnp.where(
            first_head, acc_ref[p, 0:tq, :], acc_ref[p, tq:2 * tq, :]).astype(o_ref.dtype)


def _attn_prompt(proj, kt_all, vt_all, bias, layer, batch, seq, tq):
    m = proj.shape[0]
    nq = seq // tq
    blk = SB_WIDTH
    kv_spec = pl.BlockSpec((None, None, blk, seq), lambda b, i: (layer, b, 0, 0))
    return pl.pallas_call(
        functools.partial(_attn_prompt_kernel, tq=tq),
        grid=(batch, nq),
        in_specs=[pl.BlockSpec(memory_space=pltpu.SMEM),
                  pl.BlockSpec((tq, blk), lambda b, i: (b * nq + i, COL_Q // blk)),
                  kv_spec, kv_spec,
                  pl.BlockSpec((tq, tq), lambda b, i: (0, 0))],
        out_specs=pl.BlockSpec((tq, blk), lambda b, i: (b * nq + i, 0)),
        out_shape=jax.ShapeDtypeStruct((m, blk), BF16),
        scratch_shapes=[pltpu.VMEM((nq, blk, tq), BF16),
                        pltpu.VMEM((nq, blk, tq), BF16),
                        pltpu.VMEM((SB_HEADS // 2, 2 * tq, LANES), BF16),
                        pltpu.VMEM((SB_HEADS // 2, 2 * tq, LANES), F32),
                        pltpu.VMEM((SB_HEADS // 2, 2 * tq, LANES), F32)],
        compiler_params=_cparams(("arbitrary", "arbitrary")),
        name="attn_prompt",
    )(bias, proj, kt_all, vt_all, _tri_incl(tq))


def _same_head(rows, n_new):
    rh = lax.broadcasted_iota(jnp.int32, (rows, SB_WIDTH), 0) >> (n_new.bit_length() - 1)
    ch = lax.broadcasted_iota(jnp.int32, (rows, SB_WIDTH), 1) >> (SB_HEAD_DIM.bit_length() - 1)
    return rh == ch


def _attn_sample_kernel(pt_ref, brow_ref, q_ref, kn_ref, vn_ref, tri_ref, *rest, pages_per_step, n_new):
    k_refs = rest[:pages_per_step]
    v_refs = rest[pages_per_step:2 * pages_per_step]
    o_ref = rest[2 * pages_per_step]
    qbd_ref, acc_ref, car_ref = rest[2 * pages_per_step + 1:]
    s = pl.program_id(1)
    rows = SB_HEADS * n_new
    page = tri_ref.shape[1]
    tri = tri_ref[...]
    brow = brow_ref[...]

    @pl.when(s == 0)
    def _():
        q = q_ref[...] * (1.0 / math.sqrt(SB_HEAD_DIM))
        qt = jnp.concatenate([q] * SB_HEADS, axis=0)
        qbd = jnp.where(_same_head(rows, n_new), qt, 0.0).astype(BF16)
        qbd_ref[...] = qbd
        kn = _pad_rows(kn_ref[...], page).astype(BF16)
        vn = _pad_rows(vn_ref[...], page).astype(BF16)
        qi = lax.broadcasted_iota(jnp.int32, (rows, page), 0) & (n_new - 1)
        kj = lax.broadcasted_iota(jnp.int32, (rows, page), 1)
        w, ssum = _sb_tile(_dot_nt(qbd, kn) + brow, 0.0, tri, kj < qi)
        acc_ref[...] = jnp.dot(w.astype(BF16), vn, preferred_element_type=F32)
        car_ref[...] = jnp.broadcast_to(ssum, car_ref.shape)

    qbd = qbd_ref[...]
    z = jnp.concatenate([jnp.dot(qbd, k_refs[j][...].astype(BF16), preferred_element_type=F32)
                         for j in range(pages_per_step)], axis=0)
    z = z + jnp.concatenate([brow] * pages_per_step, axis=0)

    sp = jnp.maximum(z, 0.0) + jnp.log(1.0 + jnp.exp(-jnp.abs(z)))
    cum = jnp.dot(sp.astype(BF16), tri, preferred_element_type=F32)
    ssum = jnp.sum(sp, axis=-1, keepdims=True)
    car = car_ref[...]
    cars = []
    for j in range(pages_per_step):
        cars.append(car)
        car = car + ssum[rows * j:rows * (j + 1)]
    car_ref[...] = car
    w = jnp.exp(z - cum - jnp.concatenate(cars, axis=0)).astype(BF16)
    acc = acc_ref[...]
    for j in range(pages_per_step):
        acc = acc + _dot_nt(w[rows * j:rows * (j + 1)], v_refs[j][...].astype(BF16))
    acc_ref[...] = acc

    @pl.when(s == pl.num_programs(1) - 1)
    def _():
        full = jnp.where(_same_head(rows, n_new), acc_ref[...], 0.0)
        out = full[0:n_new]
        for h in range(1, SB_HEADS):
            out = out + full[h * n_new:(h + 1) * n_new]
        o_ref[...] = out.astype(o_ref.dtype)


def _attn_sample(proj, k_new, v_new, bias, cache_k, cache_v, page_table, layer, batch, n_new, pages_per_step):
    page = cache_k.shape[3]
    n_pages = page_table.shape[0] // batch
    n_steps = n_pages // pages_per_step
    rows = SB_HEADS * n_new
    blk = SB_WIDTH
    tri = _tri_incl(page)
    brow = jnp.broadcast_to(jnp.repeat(bias, n_new)[:, None], (rows, page))

    def page_map(j):
        def index_map(b, s, pt):
            return (layer, pt[b * n_pages + (n_pages - 1 - (s * pages_per_step + j))], 0, 0)
        return index_map

    page_specs = [pl.BlockSpec((None, None, blk, page), page_map(j)) for j in range(pages_per_step)]
    const = lambda shape: pl.BlockSpec(shape, lambda b, s, pt: (0, 0))
    grid_spec = pltpu.PrefetchScalarGridSpec(
        num_scalar_prefetch=1,
        grid=(batch, n_steps),
        in_specs=[const((rows, page)),
                  pl.BlockSpec((n_new, blk), lambda b, s, pt: (b, COL_Q // blk)),
                  pl.BlockSpec((n_new, blk), lambda b, s, pt: (b, 0)),
                  pl.BlockSpec((n_new, blk), lambda b, s, pt: (b, 0)),
                  const((page, page))]
                 + page_specs + page_specs,
        out_specs=pl.BlockSpec((n_new, blk), lambda b, s, pt: (b, 0)),
        scratch_shapes=[pltpu.VMEM((rows, blk), BF16),
                        pltpu.VMEM((rows, blk), F32),
                        pltpu.VMEM((rows, page), F32)],
    )
    return pl.pallas_call(
        functools.partial(_attn_sample_kernel, pages_per_step=pages_per_step, n_new=n_new),
        grid_spec=grid_spec,
        out_shape=jax.ShapeDtypeStruct((batch * n_new, blk), F32),
        compiler_params=_cparams(("arbitrary", "arbitrary")),
        name="attn_sample",
    )(page_table, brow, proj, k_new, v_new, tri,
      *([cache_k] * pages_per_step), *([cache_v] * pages_per_step))


def _ssd_kernel(xbc_ref, z_ref, dt_ref, hist_ref, s0_ref, cw_ref, cb_ref, dtb_ref, alog_ref,
                dskip_ref, gain_ref, tri_ref, eexp_ref, y_ref, snew_ref, xh_scr, st_scr,
                *, rows_in):
    q = SSD_CHUNK
    c = pl.program_id(1)
    n_state = SSD_STATE
    gw = SSD_WIDTH // SSD_GROUPS

    @pl.when(c == 0)
    def _():
        xh_scr[0:SUBLANES, :] = hist_ref[...]
        for j in range(SSD_WIDTH // LANES):
            st_scr[:, LANES * j:LANES * (j + 1)] = s0_ref[LANES * j:LANES * (j + 1), :].T

    x = _pad_rows(xbc_ref[...], q)
    xh_scr[SUBLANES:SUBLANES + q, :] = x
    cw = cw_ref[...]
    conv = cb_ref[...] + x * cw[SSD_CONV - 1:SSD_CONV]
    for k in range(1, SSD_CONV):
        conv = conv + xh_scr[SUBLANES - k:SUBLANES - k + q, :] * cw[SSD_CONV - 1 - k:SSD_CONV - k]
    xh_scr[0:SUBLANES, :] = xh_scr[q:q + SUBLANES, :]
    xc = _silu(conv)
    xs = xc[:, :SSD_WIDTH]

    dt = _softplus(_pad_rows(dt_ref[...], q) + dtb_ref[...])
    if rows_in < q:
        rvalid = lax.broadcasted_iota(jnp.int32, (q, LANES), 0) < rows_in
        dt = jnp.where(rvalid, dt, 0.0)
    a = -jnp.exp(alog_ref[...])
    tri = tri_ref[...]
    cum = _dot_sel_lhs(tri, dt * a, 3)
    cum_t = cum.T
    dt_t = dt.T
    eexp = eexp_ref[...]
    cum_x = _dot_sel_rhs(cum, eexp, 3)
    dt_x = _dot_sel_rhs(dt, eexp, 3)
    cend_x = cum_x[q - 1:q, :]
    to_end_x = jnp.exp(cend_x - cum_x) * dt_x
    xw = (xs * to_end_x).astype(BF16)

    tt = lax.broadcasted_iota(jnp.int32, (q, q), 0)
    ss = lax.broadcasted_iota(jnp.int32, (q, q), 1)
    causal = ss <= tt
    lane = lax.broadcasted_iota(jnp.int32, (1, LANES), 1)
    first_head = lane < SSD_HEAD_DIM

    pairs_per_group = gw // LANES
    y_groups = []
    for g in range(SSD_GROUPS):
        bg = xc[:, SSD_WIDTH + g * n_state:SSD_WIDTH + (g + 1) * n_state]
        cg = xc[:, SSD_WIDTH + (SSD_GROUPS + g) * n_state:SSD_WIDTH + (SSD_GROUPS + g + 1) * n_state]
        cg16 = cg.astype(BF16)
        gcols = slice(gw * g, gw * (g + 1))
        y_off = jnp.dot(cg16, st_scr[:, gcols].astype(BF16), preferred_element_type=F32)
        cb = _dot_nt(cg16, bg.astype(BF16))
        y_pairs = []
        for jj in range(pairs_per_group):
            j = g * pairs_per_group + jj
            xs2 = xs[:, LANES * j:LANES * (j + 1)].astype(BF16)
            halves = []
            for hl in range(2):
                h = 2 * j + hl
                seg = cum[:, h:h + 1] - cum_t[h:h + 1, :]
                decay = jnp.exp(jnp.where(causal, seg, -1e30))
                wts = cb * decay * dt_t[h:h + 1, :]
                halves.append(jnp.dot(wts.astype(BF16), xs2, preferred_element_type=F32))
            y_pairs.append(jnp.where(first_head, halves[0], halves[1]))
        y_groups.append(jnp.concatenate(y_pairs, axis=1) + y_off * jnp.exp(cum_x[:, gcols]))
        st_scr[:, gcols] = (st_scr[:, gcols] * jnp.exp(cend_x[:, gcols])
                            + jnp.dot(bg.T.astype(BF16), xw[:, gcols], preferred_element_type=F32))

    y = jnp.concatenate(y_groups, axis=1) + dskip_ref[...] * xs
    y = _rms(y * _silu(_pad_rows(z_ref[...], q)), gain_ref[...])
    y_ref[...] = y[0:rows_in].astype(y_ref.dtype)

    @pl.when(c == pl.num_programs(1) - 1)
    def _():
        for j in range(SSD_WIDTH // LANES):
            snew_ref[LANES * j:LANES * (j + 1), :] = st_scr[:, LANES * j:LANES * (j + 1)].T


def _ssd(proj, dt_raw, hist, s0, conv_w, conv_b, dt_bias, a_log, d_skip, gain, batch, seq, out_dtype):
    q = SSD_CHUNK
    m = proj.shape[0]
    rows_in = q if seq % q == 0 else seq
    nc = seq // rows_in
    tri = (lax.broadcasted_iota(jnp.int32, (q, q), 1)
           <= lax.broadcasted_iota(jnp.int32, (q, q), 0)).astype(BF16)
    eexp = (lax.broadcasted_iota(jnp.int32, (LANES, SSD_WIDTH), 0)
            == lax.broadcasted_iota(jnp.int32, (LANES, SSD_WIDTH), 1) // SSD_HEAD_DIM).astype(BF16)
    pad_h = LANES - SSD_HEADS
    cw = jnp.pad(conv_w, ((0, SUBLANES - SSD_CONV), (0, 0)))
    const = lambda shape: pl.BlockSpec(shape, lambda b, c: (0,) * len(shape))
    return pl.pallas_call(
        functools.partial(_ssd_kernel, rows_in=rows_in),
        grid=(batch, nc),
        in_specs=[pl.BlockSpec((rows_in, SSD_CONV_DIM), lambda b, c: (b * nc + c, COL_XBC // SSD_CONV_DIM)),
                  pl.BlockSpec((rows_in, SSD_WIDTH), lambda b, c: (b * nc + c, COL_Z // SSD_WIDTH)),
                  pl.BlockSpec((rows_in, LANES), lambda b, c: (b * nc + c, 0)),
                  pl.BlockSpec((None, SUBLANES, SSD_CONV_DIM), lambda b, c: (b, 0, 0)),
                  pl.BlockSpec((None, SSD_WIDTH, SSD_STATE), lambda b, c: (b, 0, 0)),
                  const((SUBLANES, SSD_CONV_DIM)),
                  const((1, SSD_CONV_DIM)),
                  const((1, LANES)),
                  const((1, LANES)),
                  const((1, SSD_WIDTH)),
                  const((1, SSD_WIDTH)),
                  const((q, q)),
                  const((LANES, SSD_WIDTH))],
        out_specs=[pl.BlockSpec((rows_in, SSD_WIDTH), lambda b, c: (b * nc + c, 0)),
                   pl.BlockSpec((None, SSD_WIDTH, SSD_STATE), lambda b, c: (b, 0, 0))],
        out_shape=[jax.ShapeDtypeStruct((m, SSD_WIDTH), out_dtype),
                   jax.ShapeDtypeStruct((batch, SSD_WIDTH, SSD_STATE), F32)],
        scratch_shapes=[pltpu.VMEM((SUBLANES + q + SUBLANES, SSD_CONV_DIM), F32),
                        pltpu.VMEM((SSD_STATE, SSD_WIDTH), F32)],
        compiler_params=_cparams(("parallel", "arbitrary")),
        name="ssd",
    )(proj, proj, dt_raw, hist, s0, cw, conv_b.reshape(1, -1),
      jnp.pad(dt_bias, (0, pad_h)).reshape(1, LANES), jnp.pad(a_log, (0, pad_h)).reshape(1, LANES),
      jnp.repeat(d_skip, SSD_HEAD_DIM).reshape(1, SSD_WIDTH), gain.reshape(1, SSD_WIDTH), tri, eexp)


def _pool_kernel(x_ref, hist_ref, w_ref, scale_ref, y_ref, buf, *, tm, pos0, carry):
    hist_rows = 2 * SUBLANES
    t = pl.program_id(1)

    @pl.when(t == 0)
    def _():
        buf[0:hist_rows, :] = hist_ref[...]

    x = x_ref[...]
    buf[hist_rows:hist_rows + tm, :] = x
    pos = pos0 + t * tm + lax.broadcasted_iota(jnp.int32, (tm, 1), 0)
    for g, win in enumerate(POOL_WINDOWS):
        cols = slice(POOL_GROUP_DIM * g, POOL_GROUP_DIM * (g + 1))
        xg = x[:, cols]
        wsum = xg
        for k in range(1, win):
            wsum = wsum + buf[hist_rows - k:hist_rows - k + tm, cols]
        count = jnp.minimum(win, pos + 1).astype(F32)
        diff = wsum / count - xg
        yg = jnp.dot(diff.astype(BF16), w_ref[g], preferred_element_type=F32) * scale_ref[:, cols]
        y_ref[:, cols] = yg.astype(y_ref.dtype)
    if carry:
        buf[0:hist_rows, :] = buf[tm:tm + hist_rows, :]


def _pool(proj, hist, w, scale, batch, seq, tm, pos0, out_dtype):
    m = proj.shape[0]
    nt = seq // tm
    blk = POOL_WIDTH
    return pl.pallas_call(
        functools.partial(_pool_kernel, tm=tm, pos0=pos0, carry=nt > 1),
        grid=(batch, nt),
        in_specs=[pl.BlockSpec((tm, blk), lambda b, t: (b * nt + t, COL_POOL // blk)),
                  pl.BlockSpec((None, 2 * SUBLANES, blk), lambda b, t: (b, 0, 0)),
                  pl.BlockSpec((len(POOL_WINDOWS), POOL_GROUP_DIM, POOL_GROUP_DIM), lambda b, t: (0, 0, 0)),
                  pl.BlockSpec((1, blk), lambda b, t: (0, 0))],
        out_specs=pl.BlockSpec((tm, blk), lambda b, t: (b * nt + t, 0)),
        out_shape=jax.ShapeDtypeStruct((m, blk), out_dtype),
        scratch_shapes=[pltpu.VMEM((2 * SUBLANES + tm, blk), F32)],
        compiler_params=_cparams(("parallel", "arbitrary")),
        name="pool",
    )(proj, hist, w, scale.reshape(1, blk))


def _outproj_kernel(o_ref, y_ref, p_ref, x_ref, w_ref, g_ref, xn_ref, h_ref):
    a, b = SB_WIDTH, SB_WIDTH + SSD_WIDTH
    acc = x_ref[...]
    acc = acc + jnp.dot(o_ref[...].astype(BF16), w_ref[0:a, :], preferred_element_type=F32)
    acc = acc + jnp.dot(y_ref[...].astype(BF16), w_ref[a:b, :], preferred_element_type=F32)
    acc = acc + jnp.dot(p_ref[...].astype(BF16), w_ref[b:, :], preferred_element_type=F32)
    xn_ref[...] = acc
    h_ref[...] = _rms(acc, g_ref[...]).astype(h_ref.dtype)


def _outproj(o, y, p, x, w, layer, gain, tm):
    m, d = x.shape
    row = lambda width: pl.BlockSpec((tm, width), lambda i: (i, 0))
    return pl.pallas_call(
        _outproj_kernel,
        grid=(m // tm,),
        in_specs=[row(SB_WIDTH), row(SSD_WIDTH), row(POOL_WIDTH), row(d),
                  pl.BlockSpec((None,) + w.shape[1:], lambda i: (layer, 0, 0)),
                  pl.BlockSpec((1, d), lambda i: (0, 0))],
        out_specs=[row(d), row(d)],
        out_shape=[jax.ShapeDtypeStruct((m, d), F32), jax.ShapeDtypeStruct((m, d), BF16)],
        compiler_params=_cparams(("parallel",)),
        name="outproj",
    )(o, y, p, x, w, gain.reshape(1, d))


def _ffn_up_kernel(h_ref, wg_ref, wu_ref, hist_ref, cw_ref, cb_ref, act_ref, tail_ref, buf, wg16, wu16,
                   *, seg_rows, n_seg, tiles_per_seq):
    @pl.when(pl.program_id(1) == 0)
    def _():
        wg16[...] = wg_ref[...].astype(BF16)
        wu16[...] = wu_ref[...].astype(BF16)

    h = h_ref[...]
    g = jnp.dot(h, wg16[...], preferred_element_type=F32)
    u = jnp.dot(h, wu16[...], preferred_element_type=F32)
    cw = cw_ref[...]
    cb = cb_ref[...]

    def conv_rows(gs):
        buf[SUBLANES:SUBLANES + seg_rows, :] = gs
        out = cb + gs * cw[FFN_CONV - 1:FFN_CONV]
        for k in range(1, FFN_CONV):
            out = out + buf[SUBLANES - k:SUBLANES - k + seg_rows, :] * cw[FFN_CONV - 1 - k:FFN_CONV - k]
        return out

    if n_seg == 1:
        @pl.when(pl.program_id(1) % tiles_per_seq == 0)
        def _():
            buf[0:SUBLANES, :] = hist_ref[...]
        gc = conv_rows(g)
        buf[0:SUBLANES, :] = buf[seg_rows:seg_rows + SUBLANES, :]
        tail_ref[...] = g[seg_rows - SUBLANES:seg_rows]
        act_ref[...] = (_silu(gc) * u).astype(act_ref.dtype)
    else:
        for s in range(n_seg):
            rows = slice(seg_rows * s, seg_rows * (s + 1))
            buf[0:SUBLANES, :] = hist_ref[s]
            gc = conv_rows(g[rows])
            act_ref[rows, :] = (_silu(gc) * u[rows]).astype(act_ref.dtype)
        tail_ref[...] = g


def _ffn_up(h, wg, wu, layer, hist, conv_w, conv_b, batch, seq, tm, tn):
    m, d = h.shape
    dff = wg.shape[2]
    nt = m // tm
    cw = jnp.pad(conv_w, ((0, SUBLANES - FFN_CONV), (0, 0)))
    if tm <= seq:
        n_seg, seg_rows, tps = 1, tm, seq // tm
        hist_spec = pl.BlockSpec((None, SUBLANES, tn), lambda j, i: (i // tps, 0, j))
        tail_spec = pl.BlockSpec((None, SUBLANES, tn), lambda j, i: (i, 0, j))
        tail_shape = jax.ShapeDtypeStruct((nt, SUBLANES, dff), F32)
    else:
        n_seg, seg_rows, tps = tm // seq, seq, 1
        hist_spec = pl.BlockSpec((n_seg, SUBLANES, tn), lambda j, i: (i, 0, j))
        tail_spec = pl.BlockSpec((tm, tn), lambda j, i: (i, j))
        tail_shape = jax.ShapeDtypeStruct((m, dff), F32)
    return pl.pallas_call(
        functools.partial(_ffn_up_kernel, seg_rows=seg_rows, n_seg=n_seg, tiles_per_seq=tps),
        grid=(dff // tn, nt),
        in_specs=[pl.BlockSpec((tm, d), lambda j, i: (i, 0)),
                  pl.BlockSpec((None, d, tn), lambda j, i: (layer, 0, j)),
                  pl.BlockSpec((None, d, tn), lambda j, i: (layer, 0, j)),
                  hist_spec,
                  pl.BlockSpec((SUBLANES, tn), lambda j, i: (0, j)),
                  pl.BlockSpec((1, tn), lambda j, i: (0, j))],
        out_specs=[pl.BlockSpec((tm, tn), lambda j, i: (i, j)), tail_spec],
        out_shape=[jax.ShapeDtypeStruct((m, dff), BF16), tail_shape],
        scratch_shapes=[pltpu.VMEM((SUBLANES + seg_rows + SUBLANES, tn), F32),
                        pltpu.VMEM((d, tn), BF16),
                        pltpu.VMEM((d, tn), BF16)],
        compiler_params=_cparams(("arbitrary", "arbitrary")),
        name="ffn_up",
    )(h, wg, wu, hist, cw, conv_b.reshape(1, dff))


def _ffn_down_kernel(a_ref, w_ref, x_ref, g_ref, xn_ref, h_ref, *, tn):
    j = pl.program_id(1)
    part = jnp.dot(a_ref[...], w_ref[...], preferred_element_type=F32)
    for c in range(x_ref.shape[1] // tn):
        @pl.when(j == c)
        def _():
            xn_ref[:, tn * c:tn * (c + 1)] = x_ref[:, tn * c:tn * (c + 1)] + part

    @pl.when(j == pl.num_programs(1) - 1)
    def _():
        h_ref[...] = _rms(xn_ref[...], g_ref[...]).astype(h_ref.dtype)


def _ffn_down(act, w, layer, x, gain, tm, tn, norm_dtype):
    m, d = x.shape
    dff = act.shape[1]
    return pl.pallas_call(
        functools.partial(_ffn_down_kernel, tn=tn),
        grid=(m // tm, d // tn),
        in_specs=[pl.BlockSpec((tm, dff), lambda i, j: (i, 0)),
                  pl.BlockSpec((None, dff, tn), lambda i, j: (layer, 0, j)),
                  pl.BlockSpec((tm, d), lambda i, j: (i, 0)),
                  pl.BlockSpec((1, d), lambda i, j: (0, 0))],
        out_specs=[pl.BlockSpec((tm, d), lambda i, j: (i, 0)),
                   pl.BlockSpec((tm, d), lambda i, j: (i, 0))],
        out_shape=[jax.ShapeDtypeStruct((m, d), F32), jax.ShapeDtypeStruct((m, d), norm_dtype)],
        compiler_params=_cparams(("arbitrary", "arbitrary")),
        name="ffn_down",
    )(act, w, x, gain.reshape(1, d))


def _tiles(m):
    if m >= 1024:
        return dict(inproj=1024, outproj=256, ffn_up=1024, ffn_down=512, norm=512)
    return dict(inproj=m, outproj=m, ffn_up=m, ffn_down=m, norm=m)


def _front_pad_rows(a, rows):
    return jnp.pad(a, ((0, 0), (rows - a.shape[1], 0), (0, 0)))


def _run_trunk(x, pos0, cache, states, wts, final_norm):
    batch, seq, d = x.shape
    m = batch * seq
    depth = wts["w_main"].shape[0]
    tiles = _tiles(m)
    x2 = x.reshape(m, d)
    h = _rmsnorm(x2, wts["norm1"][0], tiles["norm"], BF16)
    small = seq % SSD_CHUNK != 0
    mix_dtype = F32 if small else BF16
    outs = [[] for _ in range(4)]
    kv_transposed = cache is None
    kv_all_shape = (depth, batch, SB_WIDTH, seq) if kv_transposed else (depth, m, SB_HEADS, SB_HEAD_DIM)
    k_all = jnp.zeros(kv_all_shape, F32)
    v_all = jnp.zeros(kv_all_shape, F32)
    y_final = None
    for l in range(depth):
        res = _inproj(h, wts["w_main"], wts["w_dt"], wts["w_k"], wts["w_v"], k_all, v_all, l,
                      tiles["inproj"], seq, kv_transposed)
        proj, dt_raw, k_all, v_all = res[0], res[1], res[-2], res[-1]
        last_rows = proj.reshape(batch, seq, PROJ_WIDTH)[:, max(0, seq - POOL_HIST):]
        xbc = last_rows[:, :, COL_XBC:COL_XBC + SSD_CONV_DIM]
        xp = last_rows[:, :, COL_POOL:COL_POOL + POOL_WIDTH]
        if cache is None:
            ssm0 = jnp.zeros((batch, SSD_WIDTH, SSD_STATE), F32)
            conv_hist = jnp.zeros((batch, SSD_CONV - 1, SSD_CONV_DIM), F32)
            pool_hist = jnp.zeros((batch, POOL_HIST, POOL_WIDTH), F32)
            ffn_hist = jnp.zeros((batch, FFN_CONV - 1, wts["w_gate"].shape[2]), F32)
            o_sb = _attn_prompt(proj, k_all, v_all, wts["sb_bias"][l], l, batch, seq, 256)
        else:
            cache_k, cache_v, page_table = cache
            ssm0 = states[0][l].reshape(batch, SSD_WIDTH, SSD_STATE)
            conv_hist, pool_hist, ffn_hist = states[1][l], states[2][l], states[3][l]
            o_sb = _attn_sample(proj, res[2], res[3], wts["sb_bias"][l], cache_k, cache_v, page_table,
                                l, batch, seq, 8)

        y_ssd, ssm_new = _ssd(proj, dt_raw, _front_pad_rows(conv_hist, SUBLANES), ssm0,
                              wts["ssd_conv_w"][l], wts["ssd_conv_b"][l], wts["dt_bias"][l], wts["a_log"][l],
                              wts["d_skip"][l], wts["ssd_norm"][l], batch, seq, mix_dtype)
        y_pool = _pool(proj, _front_pad_rows(pool_hist, 2 * SUBLANES), wts["pool_w"][l], wts["pool_scale"][l],
                       batch, seq, min(seq, 512), pos0, mix_dtype)
        outs[0].append(ssm_new.reshape(batch, SSD_HEADS, SSD_HEAD_DIM, SSD_STATE))
        outs[1].append(_tail_rows(conv_hist, xbc, SSD_CONV - 1))
        outs[2].append(_tail_rows(pool_hist, xp, POOL_HIST))

        x2, h2 = _outproj(o_sb, y_ssd, y_pool, x2, wts["w_out"], l, wts["norm2"][l], tiles["outproj"])
        act, tail = _ffn_up(h2, wts["w_gate"], wts["w_up"], l, _front_pad_rows(ffn_hist, SUBLANES),
                            wts["ffn_conv_w"][l], wts["ffn_conv_b"][l], batch, seq, tiles["ffn_up"], 512)
        dff = act.shape[1]
        if tiles["ffn_up"] <= seq:
            tps = seq // tiles["ffn_up"]
            g_last = tail.reshape(batch, tps, SUBLANES, dff)[:, tps - 1, SUBLANES - (FFN_CONV - 1):]
        else:
            g_last = tail.reshape(batch, seq, dff)[:, seq - (FFN_CONV - 1):]
        outs[3].append(g_last)
        last = l == depth - 1
        gain = final_norm if last else wts["norm1"][l + 1]
        x2, hn = _ffn_down(act, wts["w_down"], l, x2, gain, tiles["ffn_down"], 512, F32 if last else BF16)
        if last:
            y_final = hn.reshape(batch, seq, d)
        else:
            h = hn
    if kv_transposed:
        to_cache = lambda a: a.reshape(depth, batch, SB_HEADS, SB_HEAD_DIM, seq).transpose(0, 1, 4, 2, 3)
    else:
        to_cache = lambda a: a.reshape(depth, batch, seq, SB_HEADS, SB_HEAD_DIM)
    return y_final, [to_cache(k_all), to_cache(v_all)] + [jnp.stack(o) for o in outs]


def kernel(x_prompt, x_sample, cache_k, cache_v, page_table, state_ssm, state_ssm_conv, state_pool, state_ffn_conv, norm1, w_in, sb_bias, ssd_conv_w, ssd_conv_b, dt_bias, a_log, d_skip, ssd_norm, pool_w, pool_scale, w_out, norm2, w_gate, w_up, ffn_conv_w, ffn_conv_b, w_down, final_norm):
    c_q, c_z = 0, 3 * SB_WIDTH
    c_xbc = c_z + SSD_WIDTH
    c_dt = c_xbc + SSD_CONV_DIM
    c_pool = c_dt + SSD_HEADS
    c_k, c_v = c_q + SB_WIDTH, c_q + 2 * SB_WIDTH
    w_t = jnp.swapaxes(w_in, 1, 2)
    w_main = jnp.concatenate([w_t[:, c_xbc:c_dt], w_t[:, c_q:c_k], w_t[:, c_z:c_xbc], w_t[:, c_pool:]],
                             axis=1).astype(BF16)
    w_dt = jnp.pad(w_t[:, c_dt:c_pool], ((0, 0), (0, LANES - SSD_HEADS), (0, 0))).astype(BF16)
    wts = dict(norm1=norm1, w_main=w_main, w_dt=w_dt, w_k=w_t[:, c_k:c_v].astype(BF16),
               w_v=w_t[:, c_v:c_z].astype(BF16), sb_bias=sb_bias, ssd_conv_w=ssd_conv_w,
               ssd_conv_b=ssd_conv_b, dt_bias=dt_bias, a_log=a_log, d_skip=d_skip, ssd_norm=ssd_norm,
               pool_w=pool_w.astype(BF16), pool_scale=pool_scale, w_out=w_out.astype(BF16), norm2=norm2,
               w_gate=w_gate, w_up=w_up, ffn_conv_w=ffn_conv_w,
               ffn_conv_b=ffn_conv_b, w_down=w_down.astype(BF16))

    y_p, st_p = _run_trunk(x_prompt, 0, None, None, wts, final_norm)

    depth, n_pool, page = cache_k.shape[:3]
    past_len = page_table.shape[1] * page
    paged = lambda c: c.transpose(0, 1, 3, 4, 2).reshape(depth, n_pool, SB_WIDTH, page)
    cache = (paged(cache_k), paged(cache_v), page_table.reshape(-1))
    y_s, st_s = _run_trunk(x_sample, past_len, cache,
                           (state_ssm, state_ssm_conv, state_pool, state_ffn_conv), wts, final_norm)
    return (y_p, y_s, *st_p, *st_s)
```

```python
import functools
import math

import jax
import jax.numpy as jnp
from jax import lax
from jax.experimental import pallas as pl
from jax.experimental.pallas import tpu as pltpu

F32 = jnp.float32
BF16 = jnp.bfloat16

RMS_EPS = 1e-6
SB_HEADS = 8
SB_HEAD_DIM = 64
SB_WIDTH = SB_HEADS * SB_HEAD_DIM
SSD_HEADS = 16
SSD_HEAD_DIM = 64
SSD_WIDTH = SSD_HEADS * SSD_HEAD_DIM
SSD_GROUPS = 2
SSD_STATE = 128
SSD_CONV = 4
SSD_CHUNK = 128
SSD_CONV_DIM = SSD_WIDTH + 2 * SSD_GROUPS * SSD_STATE
POOL_WINDOWS = (2, 4, 8, 16)
POOL_GROUP_DIM = 128
POOL_WIDTH = POOL_GROUP_DIM * len(POOL_WINDOWS)
POOL_HIST = max(POOL_WINDOWS) - 1
FFN_CONV = 3

LANES = 128
SUBLANES = 8
VMEM_LIMIT = 56 * 1024 * 1024

COL_XBC = 0
COL_Q = SSD_CONV_DIM
COL_Z = COL_Q + SB_WIDTH
COL_POOL = COL_Z + SSD_WIDTH
PROJ_WIDTH = COL_POOL + POOL_WIDTH


def _cparams(sem):
    return pltpu.CompilerParams(dimension_semantics=sem, vmem_limit_bytes=VMEM_LIMIT)


def _softplus(x):
    return jnp.maximum(x, 0.0) + jnp.log1p(jnp.exp(-jnp.abs(x)))


def _silu(x):
    return x * jax.nn.sigmoid(x)


def _split_bf16(x, n):
    parts = []
    r = x
    for i in range(n):
        p = r.astype(BF16)
        parts.append(p)
        if i + 1 < n:
            r = r - p.astype(F32)
    return parts


def _dot_sel_rhs(x, sel, n):
    acc = None
    for p in _split_bf16(x, n):
        d = jnp.dot(p, sel, preferred_element_type=F32)
        acc = d if acc is None else acc + d
    return acc


def _dot_sel_lhs(sel, x, n):
    acc = None
    for p in _split_bf16(x, n):
        d = jnp.dot(sel, p, preferred_element_type=F32)
        acc = d if acc is None else acc + d
    return acc


def _dot_nt(a, b):
    return lax.dot_general(a, b, (((1,), (1,)), ((), ())), preferred_element_type=F32)


def _pad_rows(v, rows):
    if v.shape[0] == rows:
        return v
    return jnp.concatenate([v, jnp.zeros((rows - v.shape[0], v.shape[1]), v.dtype)], axis=0)


def _tail_rows(hist, x, n):
    seq = x.shape[1]
    if seq >= n:
        return x[:, seq - n:]
    return jnp.concatenate([hist[:, seq:], x], axis=1)


def _rms(x, gain):
    return x * lax.rsqrt(jnp.mean(x * x, axis=-1, keepdims=True) + RMS_EPS) * gain


def _norm_kernel(x_ref, g_ref, o_ref):
    o_ref[...] = _rms(x_ref[...], g_ref[...]).astype(o_ref.dtype)


def _rmsnorm(x, gain, tm, out_dtype):
    m, d = x.shape
    return pl.pallas_call(
        _norm_kernel,
        grid=(m // tm,),
        in_specs=[pl.BlockSpec((tm, d), lambda i: (i, 0)),
                  pl.BlockSpec((1, d), lambda i: (0, 0))],
        out_specs=pl.BlockSpec((tm, d), lambda i: (i, 0)),
        out_shape=jax.ShapeDtypeStruct((m, d), out_dtype),
        compiler_params=_cparams(("parallel",)),
        name="rmsnorm",
    )(x, gain.reshape(1, d))


def _inproj_kernel(h_ref, w_ref, wdt_ref, wk_ref, wv_ref, kall_ref, vall_ref, o_ref, dt_ref, *kv_refs,
                   kv_transposed):
    del kall_ref, vall_ref
    h = h_ref[...]
    o_ref[...] = _dot_nt(h, w_ref[...])

    @pl.when(pl.program_id(1) == 0)
    def _():
        if kv_transposed:
            kt_ref, vt_ref = kv_refs
            dt, kt, vt = _dot_nt(h, wdt_ref[...]), _dot_nt(wk_ref[...], h), _dot_nt(wv_ref[...], h)
            dt_ref[...] = dt
            kt_ref[...] = kt
            vt_ref[...] = vt
        else:
            dt_ref[...] = _dot_nt(h, wdt_ref[...])
            kn_ref, vn_ref, k4_ref, v4_ref = kv_refs
            for w_kv, flat_ref, split_ref in ((wk_ref, kn_ref, k4_ref), (wv_ref, vn_ref, v4_ref)):
                new = _dot_nt(h, w_kv[...])
                flat_ref[...] = new
                for hd in range(SB_HEADS):
                    split_ref[:, hd, :] = new[:, SB_HEAD_DIM * hd:SB_HEAD_DIM * (hd + 1)]


def _inproj(h, w, wdt, wk, wv, k_all, v_all, layer, tm, seq, kv_transposed):
    m, d = h.shape
    n = w.shape[1]
    tn = SB_WIDTH
    wspec = lambda rows, imap: pl.BlockSpec((None, rows, d), imap)
    flat = jax.ShapeDtypeStruct((m, SB_WIDTH), F32)
    if kv_transposed:
        tps = seq // tm
        new_spec = pl.BlockSpec((None, None, SB_WIDTH, tm), lambda i, j: (layer, i // tps, 0, i % tps))
        kv_specs, kv_shapes = [new_spec, new_spec], []
    else:
        new_spec = pl.BlockSpec((None, tm, SB_HEADS, SB_HEAD_DIM), lambda i, j: (layer, i, 0, 0))
        flat_spec = pl.BlockSpec((tm, SB_WIDTH), lambda i, j: (i, 0))
        kv_specs, kv_shapes = [flat_spec, flat_spec, new_spec, new_spec], [flat, flat]
    n_out = 2 + len(kv_specs)
    return pl.pallas_call(
        functools.partial(_inproj_kernel, kv_transposed=kv_transposed),
        grid=(m // tm, n // tn),
        in_specs=[pl.BlockSpec((tm, d), lambda i, j: (i, 0)),
                  wspec(tn, lambda i, j: (layer, j, 0)),
                  wspec(LANES, lambda i, j: (layer, 0, 0)),
                  wspec(SB_WIDTH, lambda i, j: (layer, 0, 0)),
                  wspec(SB_WIDTH, lambda i, j: (layer, 0, 0)),
                  pl.BlockSpec(memory_space=pl.ANY),
                  pl.BlockSpec(memory_space=pl.ANY)],
        out_specs=[pl.BlockSpec((tm, tn), lambda i, j: (i, j)),
                   pl.BlockSpec((tm, LANES), lambda i, j: (i, 0))] + kv_specs,
        out_shape=[jax.ShapeDtypeStruct((m, n), F32),
                   jax.ShapeDtypeStruct((m, LANES), F32)] + kv_shapes
                  + [jax.ShapeDtypeStruct(k_all.shape, F32), jax.ShapeDtypeStruct(v_all.shape, F32)],
        input_output_aliases={5: n_out - 2, 6: n_out - 1},
        compiler_params=_cparams(("arbitrary", "arbitrary")),
        name="inproj",
    )(h, w, wdt, wk, wv, k_all, v_all)


def _tri_incl(n):
    return (lax.broadcasted_iota(jnp.int32, (n, n), 0) >= lax.broadcasted_iota(jnp.int32, (n, n), 1)).astype(BF16)


def _sb_tile(z, car, tri, mask):
    sp = jnp.maximum(z, 0.0) + jnp.log(1.0 + jnp.exp(-jnp.abs(z)))
    if mask is not None:
        sp = jnp.where(mask, sp, 0.0)
    cum = jnp.dot(sp.astype(BF16), tri, preferred_element_type=F32)
    w = jnp.exp(z - cum - car)
    if mask is not None:
        w = jnp.where(mask, w, 0.0)
    return w, jnp.sum(sp, axis=-1, keepdims=True)


def _attn_prompt_kernel(bias_ref, q_ref, kt_ref, vt_ref, tri_ref, o_ref, kb_ref, vb_ref, qh_ref, acc_ref, car_ref,
                        *, tq):
    i = pl.program_id(1)
    lane = lax.broadcasted_iota(jnp.int32, (1, LANES), 1)
    first_head = lane < SB_HEAD_DIM
    tri2 = tri_ref[...]
    n_pairs = SB_HEADS // 2

    @pl.when(i == 0)
    def _():
        for c in range(kt_ref.shape[1] // tq):
            kb_ref[c] = kt_ref[:, tq * c:tq * (c + 1)].astype(BF16)
            vb_ref[c] = vt_ref[:, tq * c:tq * (c + 1)].astype(BF16)

    for p in range(n_pairs):
        q2 = q_ref[:, LANES * p:LANES * (p + 1)] * (1.0 / math.sqrt(SB_HEAD_DIM))
        qh_ref[p, 0:tq, :] = jnp.where(first_head, q2, 0.0).astype(BF16)
        qh_ref[p, tq:2 * tq, :] = jnp.where(first_head, 0.0, q2).astype(BF16)
    acc_ref[...] = jnp.zeros_like(acc_ref)
    car_ref[...] = jnp.zeros_like(car_ref)

    def key_block(kb, mask):
        for p in range(n_pairs):
            pair = slice(LANES * p, LANES * (p + 1))
            s = jnp.dot(qh_ref[p], kb_ref[kb, pair, :], preferred_element_type=F32)
            z = jnp.concatenate([s[0:tq] + bias_ref[2 * p], s[tq:2 * tq] + bias_ref[2 * p + 1]], axis=0)
            car = car_ref[p]
            w, ssum = _sb_tile(z, jnp.concatenate([car] * (tq // LANES), axis=1), tri2, mask)
            acc_ref[p] += _dot_nt(w.astype(BF16), vb_ref[kb, pair, :])
            car_ref[p] = car + ssum

    row = lax.broadcasted_iota(jnp.int32, (2 * tq, tq), 0) & (tq - 1)
    col = lax.broadcasted_iota(jnp.int32, (2 * tq, tq), 1)
    key_block(i, col < row)

    def body(t, _):
        key_block(i - 1 - t, None)
        return 0

    lax.fori_loop(0, i, body, 0)
    for p in range(n_pairs):
        o_ref[:, LANES * p:LANES * (p + 1)] = jnp.where(
            first_head, acc_ref[p, 0:tq, :], acc_ref[p, tq:2 * tq, :]).astype(o_ref.dtype)


def _attn_prompt(proj, kt_all, vt_all, bias, layer, batch, seq, tq):
    m = proj.shape[0]
    nq = seq // tq
    blk = SB_WIDTH
    kv_spec = pl.BlockSpec((None, None, blk, seq), lambda b, i: (layer, b, 0, 0))
    return pl.pallas_call(
        functools.partial(_attn_prompt_kernel, tq=tq),
        grid=(batch, nq),
        in_specs=[pl.BlockSpec(memory_space=pltpu.SMEM),
                  pl.BlockSpec((tq, blk), lambda b, i: (b * nq + i, COL_Q // blk)),
                  kv_spec, kv_spec,
                  pl.BlockSpec((tq, tq), lambda b, i: (0, 0))],
        out_specs=pl.BlockSpec((tq, blk), lambda b, i: (b * nq + i, 0)),
        out_shape=jax.ShapeDtypeStruct((m, blk), BF16),
        scratch_shapes=[pltpu.VMEM((nq, blk, tq), BF16),
                        pltpu.VMEM((nq, blk, tq), BF16),
                        pltpu.VMEM((SB_HEADS // 2, 2 * tq, LANES), BF16),
                        pltpu.VMEM((SB_HEADS // 2, 2 * tq, LANES), F32),
                        pltpu.VMEM((SB_HEADS // 2, 2 * tq, LANES), F32)],
        compiler_params=_cparams(("arbitrary", "arbitrary")),
        name="attn_prompt",
    )(bias, proj, kt_all, vt_all, _tri_incl(tq))


def _same_head(rows, n_new):
    rh = lax.broadcasted_iota(jnp.int32, (rows, SB_WIDTH), 0) >> (n_new.bit_length() - 1)
    ch = lax.broadcasted_iota(jnp.int32, (rows, SB_WIDTH), 1) >> (SB_HEAD_DIM.bit_length() - 1)
    return rh == ch


def _attn_sample_kernel(pt_ref, brow_ref, q_ref, kn_ref, vn_ref, tri_ref, *rest, pages_per_step, n_new):
    k_refs = rest[:pages_per_step]
    v_refs = rest[pages_per_step:2 * pages_per_step]
    o_ref = rest[2 * pages_per_step]
    qbd_ref, acc_ref, car_ref = rest[2 * pages_per_step + 1:]
    s = pl.program_id(1)
    rows = SB_HEADS * n_new
    page = tri_ref.shape[1]
    tri = tri_ref[...]
    brow = brow_ref[...]

    @pl.when(s == 0)
    def _():
        q = q_ref[...] * (1.0 / math.sqrt(SB_HEAD_DIM))
        qt = jnp.concatenate([q] * SB_HEADS, axis=0)
        qbd = jnp.where(_same_head(rows, n_new), qt, 0.0).astype(BF16)
        qbd_ref[...] = qbd
        kn = _pad_rows(kn_ref[...], page).astype(BF16)
        vn = _pad_rows(vn_ref[...], page).astype(BF16)
        qi = lax.broadcasted_iota(jnp.int32, (rows, page), 0) & (n_new - 1)
        kj = lax.broadcasted_iota(jnp.int32, (rows, page), 1)
        w, ssum = _sb_tile(_dot_nt(qbd, kn) + brow, 0.0, tri, kj < qi)
        acc_ref[...] = jnp.dot(w.astype(BF16), vn, preferred_element_type=F32)
        car_ref[...] = jnp.broadcast_to(ssum, car_ref.shape)

    qbd = qbd_ref[...]
    z = jnp.concatenate([jnp.dot(qbd, k_refs[j][...].astype(BF16), preferred_element_type=F32)
                         for j in range(pages_per_step)], axis=0)
    z = z + jnp.concatenate([brow] * pages_per_step, axis=0)

    sp = jnp.maximum(z, 0.0) + jnp.log(1.0 + jnp.exp(-jnp.abs(z)))
    cum = jnp.dot(sp.astype(BF16), tri, preferred_element_type=F32)
    ssum = jnp.sum(sp, axis=-1, keepdims=True)
    car = car_ref[...]
    cars = []
    for j in range(pages_per_step):
        cars.append(car)
        car = car + ssum[rows * j:rows * (j + 1)]
    car_ref[...] = car
    w = jnp.exp(z - cum - jnp.concatenate(cars, axis=0)).astype(BF16)
    acc = acc_ref[...]
    for j in range(pages_per_step):
        acc = acc + _dot_nt(w[rows * j:rows * (j + 1)], v_refs[j][...].astype(BF16))
    acc_ref[...] = acc

    @pl.when(s == pl.num_programs(1) - 1)
    def _():
        full = jnp.where(_same_head(rows, n_new), acc_ref[...], 0.0)
        out = full[0:n_new]
        for h in range(1, SB_HEADS):
            out = out + full[h * n_new:(h + 1) * n_new]
        o_ref[...] = out.astype(o_ref.dtype)


def _attn_sample(proj, k_new, v_new, bias, cache_k, cache_v, page_table, layer, batch, n_new, pages_per_step):
    page = cache_k.shape[3]
    n_pages = page_table.shape[0] // batch
    n_steps = n_pages // pages_per_step
    rows = SB_HEADS * n_new
    blk = SB_WIDTH
    tri = _tri_incl(page)
    brow = jnp.broadcast_to(jnp.repeat(bias, n_new)[:, None], (rows, page))

    def page_map(j):
        def index_map(b, s, pt):
            return (layer, pt[b * n_pages + (n_pages - 1 - (s * pages_per_step + j))], 0, 0)
        return index_map

    page_specs = [pl.BlockSpec((None, None, blk, page), page_map(j)) for j in range(pages_per_step)]
    const = lambda shape: pl.BlockSpec(shape, lambda b, s, pt: (0, 0))
    grid_spec = pltpu.PrefetchScalarGridSpec(
        num_scalar_prefetch=1,
        grid=(batch, n_steps),
        in_specs=[const((rows, page)),
                  pl.BlockSpec((n_new, blk), lambda b, s, pt: (b, COL_Q // blk)),
                  pl.BlockSpec((n_new, blk), lambda b, s, pt: (b, 0)),
                  pl.BlockSpec((n_new, blk), lambda b, s, pt: (b, 0)),
                  const((page, page))]
                 + page_specs + page_specs,
        out_specs=pl.BlockSpec((n_new, blk), lambda b, s, pt: (b, 0)),
        scratch_shapes=[pltpu.VMEM((rows, blk), BF16),
                        pltpu.VMEM((rows, blk), F32),
                        pltpu.VMEM((rows, page), F32)],
    )
    return pl.pallas_call(
        functools.partial(_attn_sample_kernel, pages_per_step=pages_per_step, n_new=n_new),
        grid_spec=grid_spec,
        out_shape=jax.ShapeDtypeStruct((batch * n_new, blk), F32),
        compiler_params=_cparams(("arbitrary", "arbitrary")),
        name="attn_sample",
    )(page_table, brow, proj, k_new, v_new, tri,
      *([cache_k] * pages_per_step), *([cache_v] * pages_per_step))


def _ssd_kernel(xbc_ref, z_ref, dt_ref, hist_ref, s0_ref, cw_ref, cb_ref, dtb_ref, alog_ref,
                dskip_ref, gain_ref, tri_ref, eexp_ref, y_ref, snew_ref, xh_scr, st_scr,
                *, rows_in):
    q = SSD_CHUNK
    c = pl.program_id(1)
    n_state = SSD_STATE
    gw = SSD_WIDTH // SSD_GROUPS

    @pl.when(c == 0)
    def _():
        xh_scr[0:SUBLANES, :] = hist_ref[...]
        for j in range(SSD_WIDTH // LANES):
            st_scr[:, LANES * j:LANES * (j + 1)] = s0_ref[LANES * j:LANES * (j + 1), :].T

    x = _pad_rows(xbc_ref[...], q)
    xh_scr[SUBLANES:SUBLANES + q, :] = x
    cw = cw_ref[...]
    conv = cb_ref[...] + x * cw[SSD_CONV - 1:SSD_CONV]
    for k in range(1, SSD_CONV):
        conv = conv + xh_scr[SUBLANES - k:SUBLANES - k + q, :] * cw[SSD_CONV - 1 - k:SSD_CONV - k]
    xh_scr[0:SUBLANES, :] = xh_scr[q:q + SUBLANES, :]
    xc = _silu(conv)
    xs = xc[:, :SSD_WIDTH]

    dt = _softplus(_pad_rows(dt_ref[...], q) + dtb_ref[...])
    if rows_in < q:
        rvalid = lax.broadcasted_iota(jnp.int32, (q, LANES), 0) < rows_in
        dt = jnp.where(rvalid, dt, 0.0)
    a = -jnp.exp(alog_ref[...])
    tri = tri_ref[...]
    cum = _dot_sel_lhs(tri, dt * a, 3)
    cum_t = cum.T
    dt_t = dt.T
    eexp = eexp_ref[...]
    cum_x = _dot_sel_rhs(cum, eexp, 3)
    dt_x = _dot_sel_rhs(dt, eexp, 3)
    cend_x = cum_x[q - 1:q, :]
    to_end_x = jnp.exp(cend_x - cum_x) * dt_x
    xw = (xs * to_end_x).astype(BF16)

    tt = lax.broadcasted_iota(jnp.int32, (q, q), 0)
    ss = lax.broadcasted_iota(jnp.int32, (q, q), 1)
    causal = ss <= tt
    lane = lax.broadcasted_iota(jnp.int32, (1, LANES), 1)
    first_head = lane < SSD_HEAD_DIM

    pairs_per_group = gw // LANES
    y_groups = []
    for g in range(SSD_GROUPS):
        bg = xc[:, SSD_WIDTH + g * n_state:SSD_WIDTH + (g + 1) * n_state]
        cg = xc[:, SSD_WIDTH + (SSD_GROUPS + g) * n_state:SSD_WIDTH + (SSD_GROUPS + g + 1) * n_state]
        cg16 = cg.astype(BF16)
        gcols = slice(gw * g, gw * (g + 1))
        y_off = jnp.dot(cg16, st_scr[:, gcols].astype(BF16), preferred_element_type=F32)
        cb = _dot_nt(cg16, bg.astype(BF16))
        y_pairs = []
        for jj in range(pairs_per_group):
            j = g * pairs_per_group + jj
            xs2 = xs[:, LANES * j:LANES * (j + 1)].astype(BF16)
            halves = []
            for hl in range(2):
                h = 2 * j + hl
                seg = cum[:, h:h + 1] - cum_t[h:h + 1, :]
                decay = jnp.exp(jnp.where(causal, seg, -1e30))
                wts = cb * decay * dt_t[h:h + 1, :]
                halves.append(jnp.dot(wts.astype(BF16), xs2, preferred_element_type=F32))
            y_pairs.append(jnp.where(first_head, halves[0], halves[1]))
        y_groups.append(jnp.concatenate(y_pairs, axis=1) + y_off * jnp.exp(cum_x[:, gcols]))
        st_scr[:, gcols] = (st_scr[:, gcols] * jnp.exp(cend_x[:, gcols])
                            + jnp.dot(bg.T.astype(BF16), xw[:, gcols], preferred_element_type=F32))

    y = jnp.concatenate(y_groups, axis=1) + dskip_ref[...] * xs
    y = _rms(y * _silu(_pad_rows(z_ref[...], q)), gain_ref[...])
    y_ref[...] = y[0:rows_in].astype(y_ref.dtype)

    @pl.when(c == pl.num_programs(1) - 1)
    def _():
        for j in range(SSD_WIDTH // LANES):
            snew_ref[LANES * j:LANES * (j + 1), :] = st_scr[:, LANES * j:LANES * (j + 1)].T


def _ssd(proj, dt_raw, hist, s0, conv_w, conv_b, dt_bias, a_log, d_skip, gain, batch, seq, out_dtype):
    q = SSD_CHUNK
    m = proj.shape[0]
    rows_in = q if seq % q == 0 else seq
    nc = seq // rows_in
    tri = (lax.broadcasted_iota(jnp.int32, (q, q), 1)
           <= lax.broadcasted_iota(jnp.int32, (q, q), 0)).astype(BF16)
    eexp = (lax.broadcasted_iota(jnp.int32, (LANES, SSD_WIDTH), 0)
            == lax.broadcasted_iota(jnp.int32, (LANES, SSD_WIDTH), 1) // SSD_HEAD_DIM).astype(BF16)
    pad_h = LANES - SSD_HEADS
    cw = jnp.pad(conv_w, ((0, SUBLANES - SSD_CONV), (0, 0)))
    const = lambda shape: pl.BlockSpec(shape, lambda b, c: (0,) * len(shape))
    return pl.pallas_call(
        functools.partial(_ssd_kernel, rows_in=rows_in),
        grid=(batch, nc),
        in_specs=[pl.BlockSpec((rows_in, SSD_CONV_DIM), lambda b, c: (b * nc + c, COL_XBC // SSD_CONV_DIM)),
                  pl.BlockSpec((rows_in, SSD_WIDTH), lambda b, c: (b * nc + c, COL_Z // SSD_WIDTH)),
                  pl.BlockSpec((rows_in, LANES), lambda b, c: (b * nc + c, 0)),
                  pl.BlockSpec((None, SUBLANES, SSD_CONV_DIM), lambda b, c: (b, 0, 0)),
                  pl.BlockSpec((None, SSD_WIDTH, SSD_STATE), lambda b, c: (b, 0, 0)),
                  const((SUBLANES, SSD_CONV_DIM)),
                  const((1, SSD_CONV_DIM)),
                  const((1, LANES)),
                  const((1, LANES)),
                  const((1, SSD_WIDTH)),
                  const((1, SSD_WIDTH)),
                  const((q, q)),
                  const((LANES, SSD_WIDTH))],
        out_specs=[pl.BlockSpec((rows_in, SSD_WIDTH), lambda b, c: (b * nc + c, 0)),
                   pl.BlockSpec((None, SSD_WIDTH, SSD_STATE), lambda b, c: (b, 0, 0))],
        out_shape=[jax.ShapeDtypeStruct((m, SSD_WIDTH), out_dtype),
                   jax.ShapeDtypeStruct((batch, SSD_WIDTH, SSD_STATE), F32)],
        scratch_shapes=[pltpu.VMEM((SUBLANES + q + SUBLANES, SSD_CONV_DIM), F32),
                        pltpu.VMEM((SSD_STATE, SSD_WIDTH), F32)],
        compiler_params=_cparams(("parallel", "arbitrary")),
        name="ssd",
    )(proj, proj, dt_raw, hist, s0, cw, conv_b.reshape(1, -1),
      jnp.pad(dt_bias, (0, pad_h)).reshape(1, LANES), jnp.pad(a_log, (0, pad_h)).reshape(1, LANES),
      jnp.repeat(d_skip, SSD_HEAD_DIM).reshape(1, SSD_WIDTH), gain.reshape(1, SSD_WIDTH), tri, eexp)


def _pool_kernel(x_ref, hist_ref, w_ref, scale_ref, y_ref, buf, *, tm, pos0, carry):
    hist_rows = 2 * SUBLANES
    t = pl.program_id(1)

    @pl.when(t == 0)
    def _():
        buf[0:hist_rows, :] = hist_ref[...]

    x = x_ref[...]
    buf[hist_rows:hist_rows + tm, :] = x
    pos = pos0 + t * tm + lax.broadcasted_iota(jnp.int32, (tm, 1), 0)
    for g, win in enumerate(POOL_WINDOWS):
        cols = slice(POOL_GROUP_DIM * g, POOL_GROUP_DIM * (g + 1))
        xg = x[:, cols]
        wsum = xg
        for k in range(1, win):
            wsum = wsum + buf[hist_rows - k:hist_rows - k + tm, cols]
        count = jnp.minimum(win, pos + 1).astype(F32)
        diff = wsum / count - xg
        yg = jnp.dot(diff.astype(BF16), w_ref[g], preferred_element_type=F32) * scale_ref[:, cols]
        y_ref[:, cols] = yg.astype(y_ref.dtype)
    if carry:
        buf[0:hist_rows, :] = buf[tm:tm + hist_rows, :]


def _pool(proj, hist, w, scale, batch, seq, tm, pos0, out_dtype):
    m = proj.shape[0]
    nt = seq // tm
    blk = POOL_WIDTH
    return pl.pallas_call(
        functools.partial(_pool_kernel, tm=tm, pos0=pos0, carry=nt > 1),
        grid=(batch, nt),
        in_specs=[pl.BlockSpec((tm, blk), lambda b, t: (b * nt + t, COL_POOL // blk)),
                  pl.BlockSpec((None, 2 * SUBLANES, blk), lambda b, t: (b, 0, 0)),
                  pl.BlockSpec((len(POOL_WINDOWS), POOL_GROUP_DIM, POOL_GROUP_DIM), lambda b, t: (0, 0, 0)),
                  pl.BlockSpec((1, blk), lambda b, t: (0, 0))],
        out_specs=pl.BlockSpec((tm, blk), lambda b, t: (b * nt + t, 0)),
        out_shape=jax.ShapeDtypeStruct((m, blk), out_dtype),
        scratch_shapes=[pltpu.VMEM((2 * SUBLANES + tm, blk), F32)],
        compiler_params=_cparams(("parallel", "arbitrary")),
        name="pool",
    )(proj, hist, w, scale.reshape(1, blk))


def _outproj_kernel(o_ref, y_ref, p_ref, x_ref, w_ref, g_ref, xn_ref, h_ref, *, n_sub):
    a, b = SB_WIDTH, SB_WIDTH + SSD_WIDTH
    rows = x_ref.shape[0] // n_sub
    accs = []
    for s in range(n_sub):
        r = slice(rows * s, rows * (s + 1))
        acc = x_ref[r, :]
        acc = acc + jnp.dot(o_ref[r, :].astype(BF16), w_ref[0:a, :], preferred_element_type=F32)
        acc = acc + jnp.dot(y_ref[r, :].astype(BF16), w_ref[a:b, :], preferred_element_type=F32)
        acc = acc + jnp.dot(p_ref[r, :].astype(BF16), w_ref[b:, :], preferred_element_type=F32)
        accs.append(acc)
    for s, acc in enumerate(accs):
        r = slice(rows * s, rows * (s + 1))
        xn_ref[r, :] = acc
        h_ref[r, :] = _rms(acc, g_ref[...]).astype(h_ref.dtype)


def _outproj(o, y, p, x, w, layer, gain, tm):
    m, d = x.shape
    row = lambda width: pl.BlockSpec((tm, width), lambda i: (i, 0))
    return pl.pallas_call(
        functools.partial(_outproj_kernel, n_sub=2 if tm >= 512 else 1),
        grid=(m // tm,),
        in_specs=[row(SB_WIDTH), row(SSD_WIDTH), row(POOL_WIDTH), row(d),
                  pl.BlockSpec((None,) + w.shape[1:], lambda i: (layer, 0, 0)),
                  pl.BlockSpec((1, d), lambda i: (0, 0))],
        out_specs=[row(d), row(d)],
        out_shape=[jax.ShapeDtypeStruct((m, d), F32), jax.ShapeDtypeStruct((m, d), BF16)],
        compiler_params=_cparams(("parallel",)),
        name="outproj",
    )(o, y, p, x, w, gain.reshape(1, d))


def _ffn_conv_rows(buf, gs, cw, cb):
    rows = gs.shape[0]
    buf[SUBLANES:SUBLANES + rows, :] = gs
    out = cb + gs * cw[FFN_CONV - 1:FFN_CONV]
    for k in range(1, FFN_CONV):
        out = out + buf[SUBLANES - k:SUBLANES - k + rows, :] * cw[FFN_CONV - 1 - k:FFN_CONV - k]
    return out


def _ffn_up_seqs_kernel(h_ref, wg_ref, wu_ref, hist_ref, cw_ref, cb_ref, act_ref, tail_ref, buf,
                        *, seg_rows, n_seg):
    h = h_ref[...]
    g = jnp.dot(h, wg_ref[...].astype(BF16), preferred_element_type=F32)
    u = jnp.dot(h, wu_ref[...].astype(BF16), preferred_element_type=F32)
    for s in range(n_seg):
        rows = slice(seg_rows * s, seg_rows * (s + 1))
        buf[0:SUBLANES, :] = hist_ref[s]
        gc = _ffn_conv_rows(buf, g[rows], cw_ref[...], cb_ref[...])
        act_ref[rows, :] = (_silu(gc) * u[rows]).astype(act_ref.dtype)
    tail_ref[...] = g


def _ffn_up_stream_kernel(h_ref, wg_ref, wu_ref, hist_ref, cw_ref, cb_ref, act_ref, tail_ref, buf, wg16, wu16,
                          *, tiles_per_seq, n_sub):
    i = pl.program_id(1)

    @pl.when(i == 0)
    def _():
        wg16[...] = wg_ref[...].astype(BF16)
        wu16[...] = wu_ref[...].astype(BF16)

    @pl.when(i % tiles_per_seq == 0)
    def _():
        buf[0:SUBLANES, :] = hist_ref[...]

    rows = h_ref.shape[0] // n_sub
    gu = []
    for s in range(n_sub):
        h = h_ref[rows * s:rows * (s + 1), :]
        gu.append((jnp.dot(h, wg16[...], preferred_element_type=F32),
                   jnp.dot(h, wu16[...], preferred_element_type=F32)))
    for s, (g, u) in enumerate(gu):
        gc = _ffn_conv_rows(buf, g, cw_ref[...], cb_ref[...])
        buf[0:SUBLANES, :] = buf[rows:rows + SUBLANES, :]
        act_ref[rows * s:rows * (s + 1), :] = (_silu(gc) * u).astype(act_ref.dtype)
    tail_ref[...] = gu[-1][0][rows - SUBLANES:rows]


def _ffn_up(h, wg, wu, layer, hist, conv_w, conv_b, batch, seq, tm, tn):
    m, d = h.shape
    dff = wg.shape[2]
    nt = m // tm
    nj = dff // tn
    cw = jnp.pad(conv_w, ((0, SUBLANES - FFN_CONV), (0, 0)))
    args = (h, wg, wu, hist, cw, conv_b.reshape(1, dff))
    if tm > seq:
        n_seg = tm // seq
        return pl.pallas_call(
            functools.partial(_ffn_up_seqs_kernel, seg_rows=seq, n_seg=n_seg),
            grid=(nj, nt),
            in_specs=[pl.BlockSpec((tm, d), lambda j, i: (i, 0)),
                      pl.BlockSpec((None, d, tn), lambda j, i: (layer, 0, j)),
                      pl.BlockSpec((None, d, tn), lambda j, i: (layer, 0, j)),
                      pl.BlockSpec((n_seg, SUBLANES, tn), lambda j, i: (i, 0, j)),
                      pl.BlockSpec((SUBLANES, tn), lambda j, i: (0, j)),
                      pl.BlockSpec((1, tn), lambda j, i: (0, j))],
            out_specs=[pl.BlockSpec((tm, tn), lambda j, i: (i, j)),
                       pl.BlockSpec((tm, tn), lambda j, i: (i, j))],
            out_shape=[jax.ShapeDtypeStruct((m, dff), BF16), jax.ShapeDtypeStruct((m, dff), F32)],
            scratch_shapes=[pltpu.VMEM((SUBLANES + seq + SUBLANES, tn), F32)],
            compiler_params=_cparams(("arbitrary", "arbitrary")),
            name="ffn_up",
        )(*args)

    tps = seq // tm
    return pl.pallas_call(
        functools.partial(_ffn_up_stream_kernel, tiles_per_seq=tps, n_sub=4 if tm >= 1024 else 1),
        grid=(nj, nt),
        in_specs=[pl.BlockSpec((tm, d), lambda j, i: (i, 0)),
                  pl.BlockSpec((None, d, tn), lambda j, i: (layer, 0, j)),
                  pl.BlockSpec((None, d, tn), lambda j, i: (layer, 0, j)),
                  pl.BlockSpec((None, SUBLANES, tn), lambda j, i: (i // tps, 0, j)),
                  pl.BlockSpec((SUBLANES, tn), lambda j, i: (0, j)),
                  pl.BlockSpec((1, tn), lambda j, i: (0, j))],
        out_specs=[pl.BlockSpec((tm, tn), lambda j, i: (i, j)),
                   pl.BlockSpec((None, SUBLANES, tn), lambda j, i: (i, 0, j))],
        out_shape=[jax.ShapeDtypeStruct((m, dff), BF16), jax.ShapeDtypeStruct((nt, SUBLANES, dff), F32)],
        scratch_shapes=[pltpu.VMEM((SUBLANES + tm + SUBLANES, tn), F32),
                        pltpu.VMEM((d, tn), BF16),
                        pltpu.VMEM((d, tn), BF16)],
        compiler_params=_cparams(("arbitrary", "arbitrary")),
        name="ffn_up",
    )(*args)


def _ffn_down_kernel(a_ref, w_ref, x_ref, g_ref, xn_ref, h_ref, *, tn):
    j = pl.program_id(1)
    part = jnp.dot(a_ref[...], w_ref[...], preferred_element_type=F32)
    for c in range(x_ref.shape[1] // tn):
        @pl.when(j == c)
        def _():
            xn_ref[:, tn * c:tn * (c + 1)] = x_ref[:, tn * c:tn * (c + 1)] + part

    @pl.when(j == pl.num_programs(1) - 1)
    def _():
        h_ref[...] = _rms(xn_ref[...], g_ref[...]).astype(h_ref.dtype)


def _ffn_down(act, w, layer, x, gain, tm, tn, norm_dtype):
    m, d = x.shape
    dff = act.shape[1]
    return pl.pallas_call(
        functools.partial(_ffn_down_kernel, tn=tn),
        grid=(m // tm, d // tn),
        in_specs=[pl.BlockSpec((tm, dff), lambda i, j: (i, 0)),
                  pl.BlockSpec((None, dff, tn), lambda i, j: (layer, 0, j)),
                  pl.BlockSpec((tm, d), lambda i, j: (i, 0)),
                  pl.BlockSpec((1, d), lambda i, j: (0, 0))],
        out_specs=[pl.BlockSpec((tm, d), lambda i, j: (i, 0)),
                   pl.BlockSpec((tm, d), lambda i, j: (i, 0))],
        out_shape=[jax.ShapeDtypeStruct((m, d), F32), jax.ShapeDtypeStruct((m, d), norm_dtype)],
        compiler_params=_cparams(("arbitrary", "arbitrary")),
        name="ffn_down",
    )(act, w, x, gain.reshape(1, d))


def _tiles(m):
    if m >= 1024:
        return dict(inproj=1024, outproj=512, ffn_up=1024, ffn_down=512, norm=512)
    return dict(inproj=m, outproj=m, ffn_up=m, ffn_down=m, norm=m)


def _front_pad_rows(a, rows):
    return jnp.pad(a, ((0, 0), (rows - a.shape[1], 0), (0, 0)))


def _run_trunk(x, pos0, cache, states, wts, final_norm):
    batch, seq, d = x.shape
    m = batch * seq
    depth = wts["w_main"].shape[0]
    tiles = _tiles(m)
    x2 = x.reshape(m, d)
    h = _rmsnorm(x2, wts["norm1"][0], tiles["norm"], BF16)
    small = seq % SSD_CHUNK != 0
    mix_dtype = F32 if small else BF16
    outs = [[] for _ in range(4)]
    kv_transposed = cache is None
    kv_all_shape = (depth, batch, SB_WIDTH, seq) if kv_transposed else (depth, m, SB_HEADS, SB_HEAD_DIM)
    k_all = jnp.zeros(kv_all_shape, F32)
    v_all = jnp.zeros(kv_all_shape, F32)
    y_final = None
    for l in range(depth):
        res = _inproj(h, wts["w_main"], wts["w_dt"], wts["w_k"], wts["w_v"], k_all, v_all, l,
                      tiles["inproj"], seq, kv_transposed)
        proj, dt_raw, k_all, v_all = res[0], res[1], res[-2], res[-1]
        last_rows = proj.reshape(batch, seq, PROJ_WIDTH)[:, max(0, seq - POOL_HIST):]
        xbc = last_rows[:, :, COL_XBC:COL_XBC + SSD_CONV_DIM]
        xp = last_rows[:, :, COL_POOL:COL_POOL + POOL_WIDTH]
        if cache is None:
            ssm0 = jnp.zeros((batch, SSD_WIDTH, SSD_STATE), F32)
            conv_hist = jnp.zeros((batch, SSD_CONV - 1, SSD_CONV_DIM), F32)
            pool_hist = jnp.zeros((batch, POOL_HIST, POOL_WIDTH), F32)
            ffn_hist = jnp.zeros((batch, FFN_CONV - 1, wts["w_gate"].shape[2]), F32)
            o_sb = _attn_prompt(proj, k_all, v_all, wts["sb_bias"][l], l, batch, seq, 256)
        else:
            cache_k, cache_v, page_table = cache
            ssm0 = states[0][l].reshape(batch, SSD_WIDTH, SSD_STATE)
            conv_hist, pool_hist, ffn_hist = states[1][l], states[2][l], states[3][l]
            o_sb = _attn_sample(proj, res[2], res[3], wts["sb_bias"][l], cache_k, cache_v, page_table,
                                l, batch, seq, 16)

        y_ssd, ssm_new = _ssd(proj, dt_raw, _front_pad_rows(conv_hist, SUBLANES), ssm0,
                              wts["ssd_conv_w"][l], wts["ssd_conv_b"][l], wts["dt_bias"][l], wts["a_log"][l],
                              wts["d_skip"][l], wts["ssd_norm"][l], batch, seq, mix_dtype)
        y_pool = _pool(proj, _front_pad_rows(pool_hist, 2 * SUBLANES), wts["pool_w"][l], wts["pool_scale"][l],
                       batch, seq, min(seq, 512), pos0, mix_dtype)
        outs[0].append(ssm_new.reshape(batch, SSD_HEADS, SSD_HEAD_DIM, SSD_STATE))
        outs[1].append(_tail_rows(conv_hist, xbc, SSD_CONV - 1))
        outs[2].append(_tail_rows(pool_hist, xp, POOL_HIST))

        x2, h2 = _outproj(o_sb, y_ssd, y_pool, x2, wts["w_out"], l, wts["norm2"][l], tiles["outproj"])
        act, tail = _ffn_up(h2, wts["w_gate"], wts["w_up"], l, _front_pad_rows(ffn_hist, SUBLANES),
                            wts["ffn_conv_w"][l], wts["ffn_conv_b"][l], batch, seq, tiles["ffn_up"], 512)
        dff = act.shape[1]
        if tiles["ffn_up"] <= seq:
            tps = seq // tiles["ffn_up"]
            g_last = tail.reshape(batch, tps, SUBLANES, dff)[:, tps - 1, SUBLANES - (FFN_CONV - 1):]
        else:
            g_last = tail.reshape(batch, seq, dff)[:, seq - (FFN_CONV - 1):]
        outs[3].append(g_last)
        last = l == depth - 1
        gain = final_norm if last else wts["norm1"][l + 1]
        x2, hn = _ffn_down(act, wts["w_down"], l, x2, gain, tiles["ffn_down"], 512, F32 if last else BF16)
        if last:
            y_final = hn.reshape(batch, seq, d)
        else:
            h = hn
    if kv_transposed:
        to_cache = lambda a: a.reshape(depth, batch, SB_HEADS, SB_HEAD_DIM, seq).transpose(0, 1, 4, 2, 3)
    else:
        to_cache = lambda a: a.reshape(depth, batch, seq, SB_HEADS, SB_HEAD_DIM)
    return y_final, [to_cache(k_all), to_cache(v_all)] + [jnp.stack(o) for o in outs]


def kernel(x_prompt, x_sample, cache_k, cache_v, page_table, state_ssm, state_ssm_conv, state_pool, state_ffn_conv, norm1, w_in, sb_bias, ssd_conv_w, ssd_conv_b, dt_bias, a_log, d_skip, ssd_norm, pool_w, pool_scale, w_out, norm2, w_gate, w_up, ffn_conv_w, ffn_conv_b, w_down, final_norm):
    c_q, c_z = 0, 3 * SB_WIDTH
    c_xbc = c_z + SSD_WIDTH
    c_dt = c_xbc + SSD_CONV_DIM
    c_pool = c_dt + SSD_HEADS
    c_k, c_v = c_q + SB_WIDTH, c_q + 2 * SB_WIDTH
    w_t = jnp.swapaxes(w_in, 1, 2)
    w_main = jnp.concatenate([w_t[:, c_xbc:c_dt], w_t[:, c_q:c_k], w_t[:, c_z:c_xbc], w_t[:, c_pool:]],
                             axis=1).astype(BF16)
    w_dt = jnp.pad(w_t[:, c_dt:c_pool], ((0, 0), (0, LANES - SSD_HEADS), (0, 0))).astype(BF16)
    wts = dict(norm1=norm1, w_main=w_main, w_dt=w_dt, w_k=w_t[:, c_k:c_v].astype(BF16),
               w_v=w_t[:, c_v:c_z].astype(BF16), sb_bias=sb_bias, ssd_conv_w=ssd_conv_w,
               ssd_conv_b=ssd_conv_b, dt_bias=dt_bias, a_log=a_log, d_skip=d_skip, ssd_norm=ssd_norm,
               pool_w=pool_w.astype(BF16), pool_scale=pool_scale, w_out=w_out.astype(BF16), norm2=norm2,
               w_gate=w_gate, w_up=w_up, ffn_conv_w=ffn_conv_w,
               ffn_conv_b=ffn_conv_b, w_down=w_down.astype(BF16))

    y_p, st_p = _run_trunk(x_prompt, 0, None, None, wts, final_norm)

    depth, n_pool, page = cache_k.shape[:3]
    past_len = page_table.shape[1] * page
    paged = lambda c: c.transpose(0, 1, 3, 4, 2).reshape(depth, n_pool, SB_WIDTH, page)
    cache = (paged(cache_k), paged(cache_v), page_table.reshape(-1))
    y_s, st_s = _run_trunk(x_sample, past_len, cache,
                           (state_ssm, state_ssm_conv, state_pool, state_ffn_conv), wts, final_norm)
    return (y_p, y_s, *st_p, *st_s)
```

```python
import functools
import math

import jax
import jax.numpy as jnp
from jax import lax
from jax.experimental import pallas as pl
from jax.experimental.pallas import tpu as pltpu

F32 = jnp.float32
BF16 = jnp.bfloat16

RMS_EPS = 1e-6
SB_HEADS = 8
SB_HEAD_DIM = 64
SB_WIDTH = SB_HEADS * SB_HEAD_DIM
SSD_HEADS = 16
SSD_HEAD_DIM = 64
SSD_WIDTH = SSD_HEADS * SSD_HEAD_DIM
SSD_GROUPS = 2
SSD_STATE = 128
SSD_CONV = 4
SSD_CHUNK = 128
SSD_CONV_DIM = SSD_WIDTH + 2 * SSD_GROUPS * SSD_STATE
POOL_WINDOWS = (2, 4, 8, 16)
POOL_GROUP_DIM = 128
POOL_WIDTH = POOL_GROUP_DIM * len(POOL_WINDOWS)
POOL_HIST = max(POOL_WINDOWS) - 1
FFN_CONV = 3

LANES = 128
SUBLANES = 8
VMEM_LIMIT = 56 * 1024 * 1024

COL_XBC = 0
COL_Q = SSD_CONV_DIM
COL_Z = COL_Q + SB_WIDTH
COL_POOL = COL_Z + SSD_WIDTH
PROJ_WIDTH = COL_POOL + POOL_WIDTH


def _cparams(sem):
    return pltpu.CompilerParams(dimension_semantics=sem, vmem_limit_bytes=VMEM_LIMIT)


def _softplus(x):
    return jnp.maximum(x, 0.0) + jnp.log1p(jnp.exp(-jnp.abs(x)))


def _silu(x):
    return x * jax.nn.sigmoid(x)


def _split_bf16(x, n):
    parts = []
    r = x
    for i in range(n):
        p = r.astype(BF16)
        parts.append(p)
        if i + 1 < n:
            r = r - p.astype(F32)
    return parts


def _dot_sel_rhs(x, sel, n):
    acc = None
    for p in _split_bf16(x, n):
        d = jnp.dot(p, sel, preferred_element_type=F32)
        acc = d if acc is None else acc + d
    return acc


def _dot_sel_lhs(sel, x, n):
    acc = None
    for p in _split_bf16(x, n):
        d = jnp.dot(sel, p, preferred_element_type=F32)
        acc = d if acc is None else acc + d
    return acc


def _dot_nt(a, b):
    return lax.dot_general(a, b, (((1,), (1,)), ((), ())), preferred_element_type=F32)


def _pad_rows(v, rows):
    if v.shape[0] == rows:
        return v
    return jnp.concatenate([v, jnp.zeros((rows - v.shape[0], v.shape[1]), v.dtype)], axis=0)


def _tail_rows(hist, x, n):
    seq = x.shape[1]
    if seq >= n:
        return x[:, seq - n:]
    return jnp.concatenate([hist[:, seq:], x], axis=1)


def _rms(x, gain):
    return x * lax.rsqrt(jnp.mean(x * x, axis=-1, keepdims=True) + RMS_EPS) * gain


def _norm_kernel(x_ref, g_ref, o_ref):
    o_ref[...] = _rms(x_ref[...], g_ref[...]).astype(o_ref.dtype)


def _rmsnorm(x, gain, tm, out_dtype):
    m, d = x.shape
    return pl.pallas_call(
        _norm_kernel,
        grid=(m // tm,),
        in_specs=[pl.BlockSpec((tm, d), lambda i: (i, 0)),
                  pl.BlockSpec((1, d), lambda i: (0, 0))],
        out_specs=pl.BlockSpec((tm, d), lambda i: (i, 0)),
        out_shape=jax.ShapeDtypeStruct((m, d), out_dtype),
        compiler_params=_cparams(("parallel",)),
        name="rmsnorm",
    )(x, gain.reshape(1, d))


def _inproj_kernel(x_ref, g_ref, w_ref, wdt_ref, wk_ref, wv_ref, kall_ref, vall_ref, o_ref, dt_ref, *rest,
                   kv_transposed):
    del kall_ref, vall_ref
    kv_refs, h_scr = rest[:-1], rest[-1]

    @pl.when(pl.program_id(1) == 0)
    def _():
        h_scr[...] = _rms(x_ref[...], g_ref[...]).astype(BF16)

    h = h_scr[...]
    o_ref[...] = _dot_nt(h, w_ref[...])

    @pl.when(pl.program_id(1) == 0)
    def _():
        if kv_transposed:
            kt_ref, vt_ref = kv_refs
            dt, kt, vt = _dot_nt(h, wdt_ref[...]), _dot_nt(wk_ref[...], h), _dot_nt(wv_ref[...], h)
            dt_ref[...] = dt
            kt_ref[...] = kt
            vt_ref[...] = vt
        else:
            dt_ref[...] = _dot_nt(h, wdt_ref[...])
            kn_ref, vn_ref, k4_ref, v4_ref = kv_refs
            for w_kv, flat_ref, split_ref in ((wk_ref, kn_ref, k4_ref), (wv_ref, vn_ref, v4_ref)):
                new = _dot_nt(h, w_kv[...])
                flat_ref[...] = new
                for hd in range(SB_HEADS):
                    split_ref[:, hd, :] = new[:, SB_HEAD_DIM * hd:SB_HEAD_DIM * (hd + 1)]


def _inproj(x, gain, w, wdt, wk, wv, k_all, v_all, layer, tm, seq, kv_transposed):
    m, d = x.shape
    n = w.shape[1]
    tn = SB_WIDTH
    wspec = lambda rows, imap: pl.BlockSpec((None, rows, d), imap)
    flat = jax.ShapeDtypeStruct((m, SB_WIDTH), F32)
    if kv_transposed:
        tps = seq // tm
        new_spec = pl.BlockSpec((None, None, SB_WIDTH, tm), lambda i, j: (layer, i // tps, 0, i % tps))
        kv_specs, kv_shapes = [new_spec, new_spec], []
    else:
        new_spec = pl.BlockSpec((None, tm, SB_HEADS, SB_HEAD_DIM), lambda i, j: (layer, i, 0, 0))
        flat_spec = pl.BlockSpec((tm, SB_WIDTH), lambda i, j: (i, 0))
        kv_specs, kv_shapes = [flat_spec, flat_spec, new_spec, new_spec], [flat, flat]
    n_out = 2 + len(kv_specs)
    return pl.pallas_call(
        functools.partial(_inproj_kernel, kv_transposed=kv_transposed),
        grid=(m // tm, n // tn),
        in_specs=[pl.BlockSpec((tm, d), lambda i, j: (i, 0)),
                  pl.BlockSpec((1, d), lambda i, j: (0, 0)),
                  wspec(tn, lambda i, j: (layer, j, 0)),
                  wspec(LANES, lambda i, j: (layer, 0, 0)),
                  wspec(SB_WIDTH, lambda i, j: (layer, 0, 0)),
                  wspec(SB_WIDTH, lambda i, j: (layer, 0, 0)),
                  pl.BlockSpec(memory_space=pl.ANY),
                  pl.BlockSpec(memory_space=pl.ANY)],
        out_specs=[pl.BlockSpec((tm, tn), lambda i, j: (i, j)),
                   pl.BlockSpec((tm, LANES), lambda i, j: (i, 0))] + kv_specs,
        out_shape=[jax.ShapeDtypeStruct((m, n), F32),
                   jax.ShapeDtypeStruct((m, LANES), F32)] + kv_shapes
                  + [jax.ShapeDtypeStruct(k_all.shape, F32), jax.ShapeDtypeStruct(v_all.shape, F32)],
        input_output_aliases={6: n_out - 2, 7: n_out - 1},
        scratch_shapes=[pltpu.VMEM((tm, d), BF16)],
        compiler_params=_cparams(("arbitrary", "arbitrary")),
        name="inproj",
    )(x, gain.reshape(1, d), w, wdt, wk, wv, k_all, v_all)


def _tri_incl(n):
    return (lax.broadcasted_iota(jnp.int32, (n, n), 0) >= lax.broadcasted_iota(jnp.int32, (n, n), 1)).astype(BF16)


def _sb_tile(z, car, tri, mask):
    sp = jnp.maximum(z, 0.0) + jnp.log(1.0 + jnp.exp(-jnp.abs(z)))
    if mask is not None:
        sp = jnp.where(mask, sp, 0.0)
    cum = jnp.dot(sp.astype(BF16), tri, preferred_element_type=F32)
    w = jnp.exp(z - cum - car)
    if mask is not None:
        w = jnp.where(mask, w, 0.0)
    return w, jnp.sum(sp, axis=-1, keepdims=True)


def _attn_prompt_kernel(bias_ref, q_ref, kt_ref, vt_ref, tri_ref, o_ref, kb_ref, vb_ref, qh_ref, acc_ref, car_ref,
                        *, tq):
    i = pl.program_id(1)
    lane = lax.broadcasted_iota(jnp.int32, (1, LANES), 1)
    first_head = lane < SB_HEAD_DIM
    tri2 = tri_ref[...]
    n_pairs = SB_HEADS // 2

    @pl.when(i == 0)
    def _():
        for c in range(kt_ref.shape[1] // tq):
            kb_ref[c] = kt_ref[:, tq * c:tq * (c + 1)].astype(BF16)
            vb_ref[c] = vt_ref[:, tq * c:tq * (c + 1)].astype(BF16)

    for p in range(n_pairs):
        q2 = q_ref[:, LANES * p:LANES * (p + 1)] * (1.0 / math.sqrt(SB_HEAD_DIM))
        qh_ref[p, 0:tq, :] = jnp.where(first_head, q2, 0.0).astype(BF16)
        qh_ref[p, tq:2 * tq, :] = jnp.where(first_head, 0.0, q2).astype(BF16)
    acc_ref[...] = jnp.zeros_like(acc_ref)
    car_ref[...] = jnp.zeros_like(car_ref)

    def key_block(kb, mask):
        for p in range(n_pairs):
            pair = slice(LANES * p, LANES * (p + 1))
            s = jnp.dot(qh_ref[p], kb_ref[kb, pair, :], preferred_element_type=F32)
            z = jnp.concatenate([s[0:tq] + bias_ref[2 * p], s[tq:2 * tq] + bias_ref[2 * p + 1]], axis=0)
            car = car_ref[p]
            w, ssum = _sb_tile(z, jnp.concatenate([car] * (tq // LANES), axis=1), tri2, mask)
            acc_ref[p] += _dot_nt(w.astype(BF16), vb_ref[kb, pair, :])
            car_ref[p] = car + ssum

    row = lax.broadcasted_iota(jnp.int32, (2 * tq, tq), 0) & (tq - 1)
    col = lax.broadcasted_iota(jnp.int32, (2 * tq, tq), 1)
    key_block(i, col < row)

    def body(t, _):
        key_block(i - 1 - t, None)
        return 0

    lax.fori_loop(0, i, body, 0)
    for p in range(n_pairs):
        o_ref[:, LANES * p:LANES * (p + 1)] = jnp.where(
            first_head, acc_ref[p, 0:tq, :], acc_ref[p, tq:2 * tq, :]).astype(o_ref.dtype)


def _attn_prompt(proj, kt_all, vt_all, bias, layer, batch, seq, tq):
    m = proj.shape[0]
    nq = seq // tq
    blk = SB_WIDTH
    kv_spec = pl.BlockSpec((None, None, blk, seq), lambda b, i: (layer, b, 0, 0))
    return pl.pallas_call(
        functools.partial(_attn_prompt_kernel, tq=tq),
        grid=(batch, nq),
        in_specs=[pl.BlockSpec(memory_space=pltpu.SMEM),
                  pl.BlockSpec((tq, blk), lambda b, i: (b * nq + i, COL_Q // blk)),
                  kv_spec, kv_spec,
                  pl.BlockSpec((tq, tq), lambda b, i: (0, 0))],
        out_specs=pl.BlockSpec((tq, blk), lambda b, i: (b * nq + i, 0)),
        out_shape=jax.ShapeDtypeStruct((m, blk), BF16),
        scratch_shapes=[pltpu.VMEM((nq, blk, tq), BF16),
                        pltpu.VMEM((nq, blk, tq), BF16),
                        pltpu.VMEM((SB_HEADS // 2, 2 * tq, LANES), BF16),
                        pltpu.VMEM((SB_HEADS // 2, 2 * tq, LANES), F32),
                        pltpu.VMEM((SB_HEADS // 2, 2 * tq, LANES), F32)],
        compiler_params=_cparams(("arbitrary", "arbitrary")),
        name="attn_prompt",
    )(bias, proj, kt_all, vt_all, _tri_incl(tq))


def _same_head(rows, n_new):
    rh = lax.broadcasted_iota(jnp.int32, (rows, SB_WIDTH), 0) >> (n_new.bit_length() - 1)
    ch = lax.broadcasted_iota(jnp.int32, (rows, SB_WIDTH), 1) >> (SB_HEAD_DIM.bit_length() - 1)
    return rh == ch


def _attn_sample_kernel(pt_ref, brow_ref, q_ref, kn_ref, vn_ref, tri_ref, *rest, pages_per_step, n_new):
    k_refs = rest[:pages_per_step]
    v_refs = rest[pages_per_step:2 * pages_per_step]
    o_ref = rest[2 * pages_per_step]
    qbd_ref, acc_ref, car_ref = rest[2 * pages_per_step + 1:]
    s = pl.program_id(1)
    rows = SB_HEADS * n_new
    page = tri_ref.shape[1]
    tri = tri_ref[...]
    brow = brow_ref[...]

    @pl.when(s == 0)
    def _():
        q = q_ref[...] * (1.0 / math.sqrt(SB_HEAD_DIM))
        qt = jnp.concatenate([q] * SB_HEADS, axis=0)
        qbd = jnp.where(_same_head(rows, n_new), qt, 0.0).astype(BF16)
        qbd_ref[...] = qbd
        kn = _pad_rows(kn_ref[...], page).astype(BF16)
        vn = _pad_rows(vn_ref[...], page).astype(BF16)
        qi = lax.broadcasted_iota(jnp.int32, (rows, page), 0) & (n_new - 1)
        kj = lax.broadcasted_iota(jnp.int32, (rows, page), 1)
        w, ssum = _sb_tile(_dot_nt(qbd, kn) + brow, 0.0, tri, kj < qi)
        acc_ref[...] = jnp.dot(w.astype(BF16), vn, preferred_element_type=F32)
        car_ref[...] = jnp.broadcast_to(ssum, car_ref.shape)

    qbd = qbd_ref[...]
    z = jnp.concatenate([jnp.dot(qbd, k_refs[j][...].astype(BF16), preferred_element_type=F32)
                         for j in range(pages_per_step)], axis=0)
    z = z + jnp.concatenate([brow] * pages_per_step, axis=0)

    sp = jnp.maximum(z, 0.0) + jnp.log(1.0 + jnp.exp(-jnp.abs(z)))
    cum = jnp.dot(sp.astype(BF16), tri, preferred_element_type=F32)
    ssum = jnp.sum(sp, axis=-1, keepdims=True)
    car = car_ref[...]
    cars = []
    for j in range(pages_per_step):
        cars.append(car)
        car = car + ssum[rows * j:rows * (j + 1)]
    car_ref[...] = car
    w = jnp.exp(z - cum - jnp.concatenate(cars, axis=0)).astype(BF16)
    acc = acc_ref[...]
    for j in range(pages_per_step):
        acc = acc + _dot_nt(w[rows * j:rows * (j + 1)], v_refs[j][...].astype(BF16))
    acc_ref[...] = acc

    @pl.when(s == pl.num_programs(1) - 1)
    def _():
        full = jnp.where(_same_head(rows, n_new), acc_ref[...], 0.0)
        out = full[0:n_new]
        for h in range(1, SB_HEADS):
            out = out + full[h * n_new:(h + 1) * n_new]
        o_ref[...] = out.astype(o_ref.dtype)


def _attn_sample(proj, k_new, v_new, bias, cache_k, cache_v, page_table, layer, batch, n_new, pages_per_step):
    page = cache_k.shape[3]
    n_pages = page_table.shape[0] // batch
    n_steps = n_pages // pages_per_step
    rows = SB_HEADS * n_new
    blk = SB_WIDTH
    tri = _tri_incl(page)
    brow = jnp.broadcast_to(jnp.repeat(bias, n_new)[:, None], (rows, page))

    def page_map(j):
        def index_map(b, s, pt):
            return (layer, pt[b * n_pages + (n_pages - 1 - (s * pages_per_step + j))], 0, 0)
        return index_map

    page_specs = [pl.BlockSpec((None, None, blk, page), page_map(j)) for j in range(pages_per_step)]
    const = lambda shape: pl.BlockSpec(shape, lambda b, s, pt: (0, 0))
    grid_spec = pltpu.PrefetchScalarGridSpec(
        num_scalar_prefetch=1,
        grid=(batch, n_steps),
        in_specs=[const((rows, page)),
                  pl.BlockSpec((n_new, blk), lambda b, s, pt: (b, COL_Q // blk)),
                  pl.BlockSpec((n_new, blk), lambda b, s, pt: (b, 0)),
                  pl.BlockSpec((n_new, blk), lambda b, s, pt: (b, 0)),
                  const((page, page))]
                 + page_specs + page_specs,
        out_specs=pl.BlockSpec((n_new, blk), lambda b, s, pt: (b, 0)),
        scratch_shapes=[pltpu.VMEM((rows, blk), BF16),
                        pltpu.VMEM((rows, blk), F32),
                        pltpu.VMEM((rows, page), F32)],
    )
    return pl.pallas_call(
        functools.partial(_attn_sample_kernel, pages_per_step=pages_per_step, n_new=n_new),
        grid_spec=grid_spec,
        out_shape=jax.ShapeDtypeStruct((batch * n_new, blk), F32),
        compiler_params=_cparams(("arbitrary", "arbitrary")),
        name="attn_sample",
    )(page_table, brow, proj, k_new, v_new, tri,
      *([cache_k] * pages_per_step), *([cache_v] * pages_per_step))


def _ssd_kernel(xbc_ref, z_ref, dt_ref, hist_ref, s0_ref, cw_ref, cb_ref, dtb_ref, alog_ref,
                dskip_ref, gain_ref, tri_ref, eexp_ref, y_ref, snew_ref, xh_scr, st_scr,
                *, rows_in):
    q = SSD_CHUNK
    c = pl.program_id(1)
    n_state = SSD_STATE
    gw = SSD_WIDTH // SSD_GROUPS

    @pl.when(c == 0)
    def _():
        xh_scr[0:SUBLANES, :] = hist_ref[...]
        for j in range(SSD_WIDTH // LANES):
            st_scr[:, LANES * j:LANES * (j + 1)] = s0_ref[LANES * j:LANES * (j + 1), :].T

    x = _pad_rows(xbc_ref[...], q)
    xh_scr[SUBLANES:SUBLANES + q, :] = x
    cw = cw_ref[...]
    conv = cb_ref[...] + x * cw[SSD_CONV - 1:SSD_CONV]
    for k in range(1, SSD_CONV):
        conv = conv + xh_scr[SUBLANES - k:SUBLANES - k + q, :] * cw[SSD_CONV - 1 - k:SSD_CONV - k]
    xh_scr[0:SUBLANES, :] = xh_scr[q:q + SUBLANES, :]
    xc = _silu(conv)
    xs = xc[:, :SSD_WIDTH]

    dt = _softplus(_pad_rows(dt_ref[...], q) + dtb_ref[...])
    if rows_in < q:
        rvalid = lax.broadcasted_iota(jnp.int32, (q, LANES), 0) < rows_in
        dt = jnp.where(rvalid, dt, 0.0)
    a = -jnp.exp(alog_ref[...])
    tri = tri_ref[...]
    cum = _dot_sel_lhs(tri, dt * a, 3)
    cum_t = cum.T
    dt_t = dt.T
    eexp = eexp_ref[...]
    cum_x = _dot_sel_rhs(cum, eexp, 3)
    dt_x = _dot_sel_rhs(dt, eexp, 3)
    cend_x = cum_x[q - 1:q, :]
    to_end_x = jnp.exp(cend_x - cum_x) * dt_x
    xw = (xs * to_end_x).astype(BF16)

    tt = lax.broadcasted_iota(jnp.int32, (q, q), 0)
    ss = lax.broadcasted_iota(jnp.int32, (q, q), 1)
    causal = ss <= tt
    lane = lax.broadcasted_iota(jnp.int32, (1, LANES), 1)
    first_head = lane < SSD_HEAD_DIM

    pairs_per_group = gw // LANES
    y_groups = []
    for g in range(SSD_GROUPS):
        bg = xc[:, SSD_WIDTH + g * n_state:SSD_WIDTH + (g + 1) * n_state]
        cg = xc[:, SSD_WIDTH + (SSD_GROUPS + g) * n_state:SSD_WIDTH + (SSD_GROUPS + g + 1) * n_state]
        cg16 = cg.astype(BF16)
        gcols = slice(gw * g, gw * (g + 1))
        y_off = jnp.dot(cg16, st_scr[:, gcols].astype(BF16), preferred_element_type=F32)
        cb = _dot_nt(cg16, bg.astype(BF16))
        y_pairs = []
        for jj in range(pairs_per_group):
            j = g * pairs_per_group + jj
            xs2 = xs[:, LANES * j:LANES * (j + 1)].astype(BF16)
            halves = []
            for hl in range(2):
                h = 2 * j + hl
                seg = cum[:, h:h + 1] - cum_t[h:h + 1, :]
                decay = jnp.exp(jnp.where(causal, seg, -1e30))
                wts = cb * decay * dt_t[h:h + 1, :]
                halves.append(jnp.dot(wts.astype(BF16), xs2, preferred_element_type=F32))
            y_pairs.append(jnp.where(first_head, halves[0], halves[1]))
        y_groups.append(jnp.concatenate(y_pairs, axis=1) + y_off * jnp.exp(cum_x[:, gcols]))
        st_scr[:, gcols] = (st_scr[:, gcols] * jnp.exp(cend_x[:, gcols])
                            + jnp.dot(bg.T.astype(BF16), xw[:, gcols], preferred_element_type=F32))

    y = jnp.concatenate(y_groups, axis=1) + dskip_ref[...] * xs
    y = _rms(y * _silu(_pad_rows(z_ref[...], q)), gain_ref[...])
    y_ref[...] = y[0:rows_in].astype(y_ref.dtype)

    @pl.when(c == pl.num_programs(1) - 1)
    def _():
        for j in range(SSD_WIDTH // LANES):
            snew_ref[LANES * j:LANES * (j + 1), :] = st_scr[:, LANES * j:LANES * (j + 1)].T


def _ssd(proj, dt_raw, hist, s0, conv_w, conv_b, dt_bias, a_log, d_skip, gain, batch, seq, out_dtype):
    q = SSD_CHUNK
    m = proj.shape[0]
    rows_in = q if seq % q == 0 else seq
    nc = seq // rows_in
    tri = (lax.broadcasted_iota(jnp.int32, (q, q), 1)
           <= lax.broadcasted_iota(jnp.int32, (q, q), 0)).astype(BF16)
    eexp = (lax.broadcasted_iota(jnp.int32, (LANES, SSD_WIDTH), 0)
            == lax.broadcasted_iota(jnp.int32, (LANES, SSD_WIDTH), 1) // SSD_HEAD_DIM).astype(BF16)
    pad_h = LANES - SSD_HEADS
    cw = jnp.pad(conv_w, ((0, SUBLANES - SSD_CONV), (0, 0)))
    const = lambda shape: pl.BlockSpec(shape, lambda b, c: (0,) * len(shape))
    return pl.pallas_call(
        functools.partial(_ssd_kernel, rows_in=rows_in),
        grid=(batch, nc),
        in_specs=[pl.BlockSpec((rows_in, SSD_CONV_DIM), lambda b, c: (b * nc + c, COL_XBC // SSD_CONV_DIM)),
                  pl.BlockSpec((rows_in, SSD_WIDTH), lambda b, c: (b * nc + c, COL_Z // SSD_WIDTH)),
                  pl.BlockSpec((rows_in, LANES), lambda b, c: (b * nc + c, 0)),
                  pl.BlockSpec((None, SUBLANES, SSD_CONV_DIM), lambda b, c: (b, 0, 0)),
                  pl.BlockSpec((None, SSD_WIDTH, SSD_STATE), lambda b, c: (b, 0, 0)),
                  const((SUBLANES, SSD_CONV_DIM)),
                  const((1, SSD_CONV_DIM)),
                  const((1, LANES)),
                  const((1, LANES)),
                  const((1, SSD_WIDTH)),
                  const((1, SSD_WIDTH)),
                  const((q, q)),
                  const((LANES, SSD_WIDTH))],
        out_specs=[pl.BlockSpec((rows_in, SSD_WIDTH), lambda b, c: (b * nc + c, 0)),
                   pl.BlockSpec((None, SSD_WIDTH, SSD_STATE), lambda b, c: (b, 0, 0))],
        out_shape=[jax.ShapeDtypeStruct((m, SSD_WIDTH), out_dtype),
                   jax.ShapeDtypeStruct((batch, SSD_WIDTH, SSD_STATE), F32)],
        scratch_shapes=[pltpu.VMEM((SUBLANES + q + SUBLANES, SSD_CONV_DIM), F32),
                        pltpu.VMEM((SSD_STATE, SSD_WIDTH), F32)],
        compiler_params=_cparams(("parallel", "arbitrary")),
        name="ssd",
    )(proj, proj, dt_raw, hist, s0, cw, conv_b.reshape(1, -1),
      jnp.pad(dt_bias, (0, pad_h)).reshape(1, LANES), jnp.pad(a_log, (0, pad_h)).reshape(1, LANES),
      jnp.repeat(d_skip, SSD_HEAD_DIM).reshape(1, SSD_WIDTH), gain.reshape(1, SSD_WIDTH), tri, eexp)


def _pool_kernel(x_ref, hist_ref, w_ref, scale_ref, y_ref, buf, *, tm, pos0, carry):
    hist_rows = 2 * SUBLANES
    t = pl.program_id(1)

    @pl.when(t == 0)
    def _():
        buf[0:hist_rows, :] = hist_ref[...]

    x = x_ref[...]
    buf[hist_rows:hist_rows + tm, :] = x
    pos = pos0 + t * tm + lax.broadcasted_iota(jnp.int32, (tm, 1), 0)
    for g, win in enumerate(POOL_WINDOWS):
        cols = slice(POOL_GROUP_DIM * g, POOL_GROUP_DIM * (g + 1))
        xg = x[:, cols]
        wsum = xg
        for k in range(1, win):
            wsum = wsum + buf[hist_rows - k:hist_rows - k + tm, cols]
        count = jnp.minimum(win, pos + 1).astype(F32)
        diff = wsum / count - xg
        yg = jnp.dot(diff.astype(BF16), w_ref[g], preferred_element_type=F32) * scale_ref[:, cols]
        y_ref[:, cols] = yg.astype(y_ref.dtype)
    if carry:
        buf[0:hist_rows, :] = buf[tm:tm + hist_rows, :]


def _pool(proj, hist, w, scale, batch, seq, tm, pos0, out_dtype):
    m = proj.shape[0]
    nt = seq // tm
    blk = POOL_WIDTH
    return pl.pallas_call(
        functools.partial(_pool_kernel, tm=tm, pos0=pos0, carry=nt > 1),
        grid=(batch, nt),
        in_specs=[pl.BlockSpec((tm, blk), lambda b, t: (b * nt + t, COL_POOL // blk)),
                  pl.BlockSpec((None, 2 * SUBLANES, blk), lambda b, t: (b, 0, 0)),
                  pl.BlockSpec((len(POOL_WINDOWS), POOL_GROUP_DIM, POOL_GROUP_DIM), lambda b, t: (0, 0, 0)),
                  pl.BlockSpec((1, blk), lambda b, t: (0, 0))],
        out_specs=pl.BlockSpec((tm, blk), lambda b, t: (b * nt + t, 0)),
        out_shape=jax.ShapeDtypeStruct((m, blk), out_dtype),
        scratch_shapes=[pltpu.VMEM((2 * SUBLANES + tm, blk), F32)],
        compiler_params=_cparams(("parallel", "arbitrary")),
        name="pool",
    )(proj, hist, w, scale.reshape(1, blk))


def _outproj_kernel(o_ref, y_ref, p_ref, x_ref, w_ref, g_ref, xn_ref, h_ref, *, n_sub):
    a, b = SB_WIDTH, SB_WIDTH + SSD_WIDTH
    rows = x_ref.shape[0] // n_sub
    accs = []
    for s in range(n_sub):
        r = slice(rows * s, rows * (s + 1))
        acc = x_ref[r, :]
        acc = acc + jnp.dot(o_ref[r, :].astype(BF16), w_ref[0:a, :], preferred_element_type=F32)
        acc = acc + jnp.dot(y_ref[r, :].astype(BF16), w_ref[a:b, :], preferred_element_type=F32)
        acc = acc + jnp.dot(p_ref[r, :].astype(BF16), w_ref[b:, :], preferred_element_type=F32)
        accs.append(acc)
    for s, acc in enumerate(accs):
        r = slice(rows * s, rows * (s + 1))
        xn_ref[r, :] = acc
        h_ref[r, :] = _rms(acc, g_ref[...]).astype(h_ref.dtype)


def _outproj(o, y, p, x, w, layer, gain, tm):
    m, d = x.shape
    row = lambda width: pl.BlockSpec((tm, width), lambda i: (i, 0))
    return pl.pallas_call(
        functools.partial(_outproj_kernel, n_sub=2 if tm >= 512 else 1),
        grid=(m // tm,),
        in_specs=[row(SB_WIDTH), row(SSD_WIDTH), row(POOL_WIDTH), row(d),
                  pl.BlockSpec((None,) + w.shape[1:], lambda i: (layer, 0, 0)),
                  pl.BlockSpec((1, d), lambda i: (0, 0))],
        out_specs=[row(d), row(d)],
        out_shape=[jax.ShapeDtypeStruct((m, d), F32), jax.ShapeDtypeStruct((m, d), BF16)],
        compiler_params=_cparams(("parallel",)),
        name="outproj",
    )(o, y, p, x, w, gain.reshape(1, d))


def _ffn_conv_rows(buf, gs, cw, cb):
    rows = gs.shape[0]
    buf[SUBLANES:SUBLANES + rows, :] = gs
    out = cb + gs * cw[FFN_CONV - 1:FFN_CONV]
    for k in range(1, FFN_CONV):
        out = out + buf[SUBLANES - k:SUBLANES - k + rows, :] * cw[FFN_CONV - 1 - k:FFN_CONV - k]
    return out


def _ffn_up_seqs_kernel(h_ref, wg_ref, wu_ref, hist_ref, cw_ref, cb_ref, act_ref, tail_ref, buf,
                        *, seg_rows, n_seg):
    h = h_ref[...]
    g = jnp.dot(h, wg_ref[...].astype(BF16), preferred_element_type=F32)
    u = jnp.dot(h, wu_ref[...].astype(BF16), preferred_element_type=F32)
    for s in range(n_seg):
        rows = slice(seg_rows * s, seg_rows * (s + 1))
        buf[0:SUBLANES, :] = hist_ref[s]
        gc = _ffn_conv_rows(buf, g[rows], cw_ref[...], cb_ref[...])
        act_ref[rows, :] = (_silu(gc) * u[rows]).astype(act_ref.dtype)
    tail_ref[...] = g


def _ffn_up_stream_kernel(h_ref, wg_ref, wu_ref, hist_ref, cw_ref, cb_ref, act_ref, tail_ref, buf, wg16, wu16,
                          *, tiles_per_seq, n_sub):
    i = pl.program_id(1)

    @pl.when(i == 0)
    def _():
        wg16[...] = wg_ref[...].astype(BF16)
        wu16[...] = wu_ref[...].astype(BF16)

    @pl.when(i % tiles_per_seq == 0)
    def _():
        buf[0:SUBLANES, :] = hist_ref[...]

    rows = h_ref.shape[0] // n_sub
    gu = []
    for s in range(n_sub):
        h = h_ref[rows * s:rows * (s + 1), :]
        gu.append((jnp.dot(h, wg16[...], preferred_element_type=F32),
                   jnp.dot(h, wu16[...], preferred_element_type=F32)))
    for s, (g, u) in enumerate(gu):
        gc = _ffn_conv_rows(buf, g, cw_ref[...], cb_ref[...])
        buf[0:SUBLANES, :] = buf[rows:rows + SUBLANES, :]
        act_ref[rows * s:rows * (s + 1), :] = (_silu(gc) * u).astype(act_ref.dtype)
    tail_ref[...] = gu[-1][0][rows - SUBLANES:rows]


def _ffn_up(h, wg, wu, layer, hist, conv_w, conv_b, batch, seq, tm, tn):
    m, d = h.shape
    dff = wg.shape[2]
    nt = m // tm
    nj = dff // tn
    cw = jnp.pad(conv_w, ((0, SUBLANES - FFN_CONV), (0, 0)))
    args = (h, wg, wu, hist, cw, conv_b.reshape(1, dff))
    if tm > seq:
        n_seg = tm // seq
        return pl.pallas_call(
            functools.partial(_ffn_up_seqs_kernel, seg_rows=seq, n_seg=n_seg),
            grid=(nj, nt),
            in_specs=[pl.BlockSpec((tm, d), lambda j, i: (i, 0)),
                      pl.BlockSpec((None, d, tn), lambda j, i: (layer, 0, j)),
                      pl.BlockSpec((None, d, tn), lambda j, i: (layer, 0, j)),
                      pl.BlockSpec((n_seg, SUBLANES, tn), lambda j, i: (i, 0, j)),
                      pl.BlockSpec((SUBLANES, tn), lambda j, i: (0, j)),
                      pl.BlockSpec((1, tn), lambda j, i: (0, j))],
            out_specs=[pl.BlockSpec((tm, tn), lambda j, i: (i, j)),
                       pl.BlockSpec((tm, tn), lambda j, i: (i, j))],
            out_shape=[jax.ShapeDtypeStruct((m, dff), BF16), jax.ShapeDtypeStruct((m, dff), F32)],
            scratch_shapes=[pltpu.VMEM((SUBLANES + seq + SUBLANES, tn), F32)],
            compiler_params=_cparams(("arbitrary", "arbitrary")),
            name="ffn_up",
        )(*args)

    tps = seq // tm
    return pl.pallas_call(
        functools.partial(_ffn_up_stream_kernel, tiles_per_seq=tps, n_sub=4 if tm >= 1024 else 1),
        grid=(nj, nt),
        in_specs=[pl.BlockSpec((tm, d), lambda j, i: (i, 0)),
                  pl.BlockSpec((None, d, tn), lambda j, i: (layer, 0, j)),
                  pl.BlockSpec((None, d, tn), lambda j, i: (layer, 0, j)),
                  pl.BlockSpec((None, SUBLANES, tn), lambda j, i: (i // tps, 0, j)),
                  pl.BlockSpec((SUBLANES, tn), lambda j, i: (0, j)),
                  pl.BlockSpec((1, tn), lambda j, i: (0, j))],
        out_specs=[pl.BlockSpec((tm, tn), lambda j, i: (i, j)),
                   pl.BlockSpec((None, SUBLANES, tn), lambda j, i: (i, 0, j))],
        out_shape=[jax.ShapeDtypeStruct((m, dff), BF16), jax.ShapeDtypeStruct((nt, SUBLANES, dff), F32)],
        scratch_shapes=[pltpu.VMEM((SUBLANES + tm + SUBLANES, tn), F32),
                        pltpu.VMEM((d, tn), BF16),
                        pltpu.VMEM((d, tn), BF16)],
        compiler_params=_cparams(("arbitrary", "arbitrary")),
        name="ffn_up",
    )(*args)


def _ffn_down_kernel(a_ref, w_ref, x_ref, xn_ref, *, n_sub):
    rows = a_ref.shape[0] // n_sub
    parts = [jnp.dot(a_ref[rows * s:rows * (s + 1), :], w_ref[...], preferred_element_type=F32)
             for s in range(n_sub)]
    for s, part in enumerate(parts):
        r = slice(rows * s, rows * (s + 1))
        xn_ref[r, :] = x_ref[r, :] + part


def _ffn_down(act, w, layer, x, tm, tn):
    m, d = x.shape
    dff = act.shape[1]
    return pl.pallas_call(
        functools.partial(_ffn_down_kernel, n_sub=2 if tm >= 1024 else 1),
        grid=(m // tm, d // tn),
        in_specs=[pl.BlockSpec((tm, dff), lambda i, j: (i, 0)),
                  pl.BlockSpec((None, dff, tn), lambda i, j: (layer, 0, j)),
                  pl.BlockSpec((tm, tn), lambda i, j: (i, j))],
        out_specs=pl.BlockSpec((tm, tn), lambda i, j: (i, j)),
        out_shape=jax.ShapeDtypeStruct((m, d), F32),
        compiler_params=_cparams(("arbitrary", "arbitrary")),
        name="ffn_down",
    )(act, w, x)


def _tiles(m):
    if m >= 1024:
        return dict(inproj=1024, outproj=512, ffn_up=1024, ffn_down=1024, norm=512)
    return dict(inproj=m, outproj=m, ffn_up=m, ffn_down=m, norm=m)


def _front_pad_rows(a, rows):
    return jnp.pad(a, ((0, 0), (rows - a.shape[1], 0), (0, 0)))


def _run_trunk(x, pos0, cache, states, wts, final_norm):
    batch, seq, d = x.shape
    m = batch * seq
    depth = wts["w_main"].shape[0]
    tiles = _tiles(m)
    x2 = x.reshape(m, d)
    small = seq % SSD_CHUNK != 0
    mix_dtype = F32 if small else BF16
    outs = [[] for _ in range(4)]
    kv_transposed = cache is None
    kv_all_shape = (depth, batch, SB_WIDTH, seq) if kv_transposed else (depth, m, SB_HEADS, SB_HEAD_DIM)
    k_all = jnp.zeros(kv_all_shape, F32)
    v_all = jnp.zeros(kv_all_shape, F32)
    for l in range(depth):
        res = _inproj(x2, wts["norm1"][l], wts["w_main"], wts["w_dt"], wts["w_k"], wts["w_v"], k_all, v_all, l,
                      tiles["inproj"], seq, kv_transposed)
        proj, dt_raw, k_all, v_all = res[0], res[1], res[-2], res[-1]
        last_rows = proj.reshape(batch, seq, PROJ_WIDTH)[:, max(0, seq - POOL_HIST):]
        xbc = last_rows[:, :, COL_XBC:COL_XBC + SSD_CONV_DIM]
        xp = last_rows[:, :, COL_POOL:COL_POOL + POOL_WIDTH]
        if cache is None:
            ssm0 = jnp.zeros((batch, SSD_WIDTH, SSD_STATE), F32)
            conv_hist = jnp.zeros((batch, SSD_CONV - 1, SSD_CONV_DIM), F32)
            pool_hist = jnp.zeros((batch, POOL_HIST, POOL_WIDTH), F32)
            ffn_hist = jnp.zeros((batch, FFN_CONV - 1, wts["w_gate"].shape[2]), F32)
            o_sb = _attn_prompt(proj, k_all, v_all, wts["sb_bias"][l], l, batch, seq, 256)
        else:
            cache_k, cache_v, page_table = cache
            ssm0 = states[0][l].reshape(batch, SSD_WIDTH, SSD_STATE)
            conv_hist, pool_hist, ffn_hist = states[1][l], states[2][l], states[3][l]
            o_sb = _attn_sample(proj, res[2], res[3], wts["sb_bias"][l], cache_k, cache_v, page_table,
                                l, batch, seq, 16)

        y_ssd, ssm_new = _ssd(proj, dt_raw, _front_pad_rows(conv_hist, SUBLANES), ssm0,
                              wts["ssd_conv_w"][l], wts["ssd_conv_b"][l], wts["dt_bias"][l], wts["a_log"][l],
                              wts["d_skip"][l], wts["ssd_norm"][l], batch, seq, mix_dtype)
        y_pool = _pool(proj, _front_pad_rows(pool_hist, 2 * SUBLANES), wts["pool_w"][l], wts["pool_scale"][l],
                       batch, seq, min(seq, 512), pos0, mix_dtype)
        outs[0].append(ssm_new.reshape(batch, SSD_HEADS, SSD_HEAD_DIM, SSD_STATE))
        outs[1].append(_tail_rows(conv_hist, xbc, SSD_CONV - 1))
        outs[2].append(_tail_rows(pool_hist, xp, POOL_HIST))

        x2, h2 = _outproj(o_sb, y_ssd, y_pool, x2, wts["w_out"], l, wts["norm2"][l], tiles["outproj"])
        act, tail = _ffn_up(h2, wts["w_gate"], wts["w_up"], l, _front_pad_rows(ffn_hist, SUBLANES),
                            wts["ffn_conv_w"][l], wts["ffn_conv_b"][l], batch, seq, tiles["ffn_up"], 512)
        dff = act.shape[1]
        if tiles["ffn_up"] <= seq:
            tps = seq // tiles["ffn_up"]
            g_last = tail.reshape(batch, tps, SUBLANES, dff)[:, tps - 1, SUBLANES - (FFN_CONV - 1):]
        else:
            g_last = tail.reshape(batch, seq, dff)[:, seq - (FFN_CONV - 1):]
        outs[3].append(g_last)
        x2 = _ffn_down(act, wts["w_down"], l, x2, tiles["ffn_down"], 512)
    y_final = _rmsnorm(x2, final_norm, tiles["norm"], F32).reshape(batch, seq, d)
    if kv_transposed:
        to_cache = lambda a: a.reshape(depth, batch, SB_HEADS, SB_HEAD_DIM, seq).transpose(0, 1, 4, 2, 3)
    else:
        to_cache = lambda a: a.reshape(depth, batch, seq, SB_HEADS, SB_HEAD_DIM)
    return y_final, [to_cache(k_all), to_cache(v_all)] + [jnp.stack(o) for o in outs]


def kernel(x_prompt, x_sample, cache_k, cache_v, page_table, state_ssm, state_ssm_conv, state_pool, state_ffn_conv, norm1, w_in, sb_bias, ssd_conv_w, ssd_conv_b, dt_bias, a_log, d_skip, ssd_norm, pool_w, pool_scale, w_out, norm2, w_gate, w_up, ffn_conv_w, ffn_conv_b, w_down, final_norm):
    c_q, c_z = 0, 3 * SB_WIDTH
    c_xbc = c_z + SSD_WIDTH
    c_dt = c_xbc + SSD_CONV_DIM
    c_pool = c_dt + SSD_HEADS
    c_k, c_v = c_q + SB_WIDTH, c_q + 2 * SB_WIDTH
    w_t = jnp.swapaxes(w_in, 1, 2)
    w_main = jnp.concatenate([w_t[:, c_xbc:c_dt], w_t[:, c_q:c_k], w_t[:, c_z:c_xbc], w_t[:, c_pool:]],
                             axis=1).astype(BF16)
    w_dt = jnp.pad(w_t[:, c_dt:c_pool], ((0, 0), (0, LANES - SSD_HEADS), (0, 0))).astype(BF16)
    wts = dict(norm1=norm1, w_main=w_main, w_dt=w_dt, w_k=w_t[:, c_k:c_v].astype(BF16),
               w_v=w_t[:, c_v:c_z].astype(BF16), sb_bias=sb_bias, ssd_conv_w=ssd_conv_w,
               ssd_conv_b=ssd_conv_b, dt_bias=dt_bias, a_log=a_log, d_skip=d_skip, ssd_norm=ssd_norm,
               pool_w=pool_w.astype(BF16), pool_scale=pool_scale, w_out=w_out.astype(BF16), norm2=norm2,
               w_gate=w_gate, w_up=w_up, ffn_conv_w=ffn_conv_w,
               ffn_conv_b=ffn_conv_b, w_down=w_down.astype(BF16))

    y_p, st_p = _run_trunk(x_prompt, 0, None, None, wts, final_norm)

    depth, n_pool, page = cache_k.shape[:3]
    past_len = page_table.shape[1] * page
    paged = lambda c: c.transpose(0, 1, 3, 4, 2).reshape(depth, n_pool, SB_WIDTH, page)
    cache = (paged(cache_k), paged(cache_v), page_table.reshape(-1))
    y_s, st_s = _run_trunk(x_sample, past_len, cache,
                           (state_ssm, state_ssm_conv, state_pool, state_ffn_conv), wts, final_norm)
    return (y_p, y_s, *st_p, *st_s)
```

```python
import functools
import math

import jax
import jax.numpy as jnp
from jax import lax
from jax.experimental import pallas as pl
from jax.experimental.pallas import tpu as pltpu

F32 = jnp.float32
BF16 = jnp.bfloat16

RMS_EPS = 1e-6
SB_HEADS = 8
SB_HEAD_DIM = 64
SB_WIDTH = SB_HEADS * SB_HEAD_DIM
SSD_HEADS = 16
SSD_HEAD_DIM = 64
SSD_WIDTH = SSD_HEADS * SSD_HEAD_DIM
SSD_GROUPS = 2
SSD_STATE = 128
SSD_CONV = 4
SSD_CHUNK = 128
SSD_CONV_DIM = SSD_WIDTH + 2 * SSD_GROUPS * SSD_STATE
POOL_WINDOWS = (2, 4, 8, 16)
POOL_GROUP_DIM = 128
POOL_WIDTH = POOL_GROUP_DIM * len(POOL_WINDOWS)
POOL_HIST = max(POOL_WINDOWS) - 1
FFN_CONV = 3

LANES = 128
SUBLANES = 8
VMEM_LIMIT = 56 * 1024 * 1024

COL_XBC = 0
COL_Q = SSD_CONV_DIM
COL_Z = COL_Q + SB_WIDTH
COL_POOL = COL_Z + SSD_WIDTH
PROJ_WIDTH = COL_POOL + POOL_WIDTH


def _cparams(sem):
    return pltpu.CompilerParams(dimension_semantics=sem, vmem_limit_bytes=VMEM_LIMIT)


def _softplus(x):
    return jnp.maximum(x, 0.0) + jnp.log1p(jnp.exp(-jnp.abs(x)))


def _silu(x):
    return x * jax.nn.sigmoid(x)


def _split_bf16(x, n):
    parts = []
    r = x
    for i in range(n):
        p = r.astype(BF16)
        parts.append(p)
        if i + 1 < n:
            r = r - p.astype(F32)
    return parts


def _dot_sel_rhs(x, sel, n):
    acc = None
    for p in _split_bf16(x, n):
        d = jnp.dot(p, sel, preferred_element_type=F32)
        acc = d if acc is None else acc + d
    return acc


def _dot_sel_lhs(sel, x, n):
    acc = None
    for p in _split_bf16(x, n):
        d = jnp.dot(sel, p, preferred_element_type=F32)
        acc = d if acc is None else acc + d
    return acc


def _dot_nt(a, b):
    return lax.dot_general(a, b, (((1,), (1,)), ((), ())), preferred_element_type=F32)


def _pad_rows(v, rows):
    if v.shape[0] == rows:
        return v
    return jnp.concatenate([v, jnp.zeros((rows - v.shape[0], v.shape[1]), v.dtype)], axis=0)


def _tail_rows(hist, x, n):
    seq = x.shape[1]
    if seq >= n:
        return x[:, seq - n:]
    return jnp.concatenate([hist[:, seq:], x], axis=1)


def _rms(x, gain):
    return x * lax.rsqrt(jnp.mean(x * x, axis=-1, keepdims=True) + RMS_EPS) * gain


def _norm_kernel(x_ref, g_ref, o_ref):
    o_ref[...] = _rms(x_ref[...], g_ref[...]).astype(o_ref.dtype)


def _rmsnorm(x, gain, tm, out_dtype):
    m, d = x.shape
    return pl.pallas_call(
        _norm_kernel,
        grid=(m // tm,),
        in_specs=[pl.BlockSpec((tm, d), lambda i: (i, 0)),
                  pl.BlockSpec((1, d), lambda i: (0, 0))],
        out_specs=pl.BlockSpec((tm, d), lambda i: (i, 0)),
        out_shape=jax.ShapeDtypeStruct((m, d), out_dtype),
        compiler_params=_cparams(("parallel",)),
        name="rmsnorm",
    )(x, gain.reshape(1, d))


def _inproj_kernel(x_ref, g_ref, w_ref, wdt_ref, wk_ref, wv_ref, kall_ref, vall_ref, o_ref, dt_ref, *rest,
                   kv_transposed, n_sub):
    del kall_ref, vall_ref
    kv_refs, h_scr = rest[:-1], rest[-1]

    @pl.when(pl.program_id(1) > 0)
    def _():
        o_ref[...] = _dot_nt(h_scr[...], w_ref[...])

    @pl.when(pl.program_id(1) == 0)
    def _():
        rows = x_ref.shape[0] // n_sub
        hs = [_rms(x_ref[rows * s:rows * (s + 1), :], g_ref[...]).astype(BF16) for s in range(n_sub)]
        outs = [_dot_nt(h_s, w_ref[...]) for h_s in hs]
        h = jnp.concatenate(hs, axis=0)
        for s in range(n_sub):
            o_ref[rows * s:rows * (s + 1), :] = outs[s]
        h_scr[...] = h
        if kv_transposed:
            kt_ref, vt_ref = kv_refs
            dt, kt, vt = _dot_nt(h, wdt_ref[...]), _dot_nt(wk_ref[...], h), _dot_nt(wv_ref[...], h)
            dt_ref[...] = dt
            kt_ref[...] = kt
            vt_ref[...] = vt
        else:
            dt_ref[...] = _dot_nt(h, wdt_ref[...])
            kn_ref, vn_ref, k4_ref, v4_ref = kv_refs
            for w_kv, flat_ref, split_ref in ((wk_ref, kn_ref, k4_ref), (wv_ref, vn_ref, v4_ref)):
                new = _dot_nt(h, w_kv[...])
                flat_ref[...] = new
                for hd in range(SB_HEADS):
                    split_ref[:, hd, :] = new[:, SB_HEAD_DIM * hd:SB_HEAD_DIM * (hd + 1)]


def _inproj(x, gain, w, wdt, wk, wv, k_all, v_all, layer, tm, seq, kv_transposed):
    m, d = x.shape
    n = w.shape[1]
    tn = SB_WIDTH
    wspec = lambda rows, imap: pl.BlockSpec((None, rows, d), imap)
    flat = jax.ShapeDtypeStruct((m, SB_WIDTH), F32)
    if kv_transposed:
        tps = seq // tm
        new_spec = pl.BlockSpec((None, None, SB_WIDTH, tm), lambda i, j: (layer, i // tps, 0, i % tps))
        kv_specs, kv_shapes = [new_spec, new_spec], []
    else:
        new_spec = pl.BlockSpec((None, tm, SB_HEADS, SB_HEAD_DIM), lambda i, j: (layer, i, 0, 0))
        flat_spec = pl.BlockSpec((tm, SB_WIDTH), lambda i, j: (i, 0))
        kv_specs, kv_shapes = [flat_spec, flat_spec, new_spec, new_spec], [flat, flat]
    n_out = 2 + len(kv_specs)
    return pl.pallas_call(
        functools.partial(_inproj_kernel, kv_transposed=kv_transposed, n_sub=4 if tm >= 1024 else 1),
        grid=(m // tm, n // tn),
        in_specs=[pl.BlockSpec((tm, d), lambda i, j: (i, 0)),
                  pl.BlockSpec((1, d), lambda i, j: (0, 0)),
                  wspec(tn, lambda i, j: (layer, j, 0)),
                  wspec(LANES, lambda i, j: (layer, 0, 0)),
                  wspec(SB_WIDTH, lambda i, j: (layer, 0, 0)),
                  wspec(SB_WIDTH, lambda i, j: (layer, 0, 0)),
                  pl.BlockSpec(memory_space=pl.ANY),
                  pl.BlockSpec(memory_space=pl.ANY)],
        out_specs=[pl.BlockSpec((tm, tn), lambda i, j: (i, j)),
                   pl.BlockSpec((tm, LANES), lambda i, j: (i, 0))] + kv_specs,
        out_shape=[jax.ShapeDtypeStruct((m, n), F32),
                   jax.ShapeDtypeStruct((m, LANES), F32)] + kv_shapes
                  + [jax.ShapeDtypeStruct(k_all.shape, F32), jax.ShapeDtypeStruct(v_all.shape, F32)],
        input_output_aliases={6: n_out - 2, 7: n_out - 1},
        scratch_shapes=[pltpu.VMEM((tm, d), BF16)],
        compiler_params=_cparams(("arbitrary", "arbitrary")),
        name="inproj",
    )(x, gain.reshape(1, d), w, wdt, wk, wv, k_all, v_all)


def _tri_incl(n):
    return (lax.broadcasted_iota(jnp.int32, (n, n), 0) >= lax.broadcasted_iota(jnp.int32, (n, n), 1)).astype(BF16)


def _sb_tile(z, car, tri, mask):
    sp = jnp.maximum(z, 0.0) + jnp.log(1.0 + jnp.exp(-jnp.abs(z)))
    if mask is not None:
        sp = jnp.where(mask, sp, 0.0)
    cum = jnp.dot(sp.astype(BF16), tri, preferred_element_type=F32)
    w = jnp.exp(z - cum - car)
    if mask is not None:
        w = jnp.where(mask, w, 0.0)
    return w, jnp.sum(sp, axis=-1, keepdims=True)


def _attn_prompt_kernel(bias_ref, q_ref, kt_ref, vt_ref, tri_ref, o_ref, kb_ref, vb_ref, qh_ref, acc_ref, car_ref,
                        *, tq):
    i = pl.program_id(1)
    lane = lax.broadcasted_iota(jnp.int32, (1, LANES), 1)
    first_head = lane < SB_HEAD_DIM
    tri2 = tri_ref[...]
    n_pairs = SB_HEADS // 2

    @pl.when(i == 0)
    def _():
        for c in range(kt_ref.shape[1] // tq):
            kb_ref[c] = kt_ref[:, tq * c:tq * (c + 1)].astype(BF16)
            vb_ref[c] = vt_ref[:, tq * c:tq * (c + 1)].astype(BF16)

    for p in range(n_pairs):
        q2 = q_ref[:, LANES * p:LANES * (p + 1)] * (1.0 / math.sqrt(SB_HEAD_DIM))
        qh_ref[p, 0:tq, :] = jnp.where(first_head, q2, 0.0).astype(BF16)
        qh_ref[p, tq:2 * tq, :] = jnp.where(first_head, 0.0, q2).astype(BF16)
    acc_ref[...] = jnp.zeros_like(acc_ref)
    car_ref[...] = jnp.zeros_like(car_ref)

    def key_block(kb, mask):
        for p in range(n_pairs):
            pair = slice(LANES * p, LANES * (p + 1))
            s = jnp.dot(qh_ref[p], kb_ref[kb, pair, :], preferred_element_type=F32)
            z = jnp.concatenate([s[0:tq] + bias_ref[2 * p], s[tq:2 * tq] + bias_ref[2 * p + 1]], axis=0)
            car = car_ref[p]
            w, ssum = _sb_tile(z, jnp.concatenate([car] * (tq // LANES), axis=1), tri2, mask)
            acc_ref[p] += _dot_nt(w.astype(BF16), vb_ref[kb, pair, :])
            car_ref[p] = car + ssum

    row = lax.broadcasted_iota(jnp.int32, (2 * tq, tq), 0) & (tq - 1)
    col = lax.broadcasted_iota(jnp.int32, (2 * tq, tq), 1)
    key_block(i, col < row)

    def body(t, _):
        key_block(i - 1 - t, None)
        return 0

    lax.fori_loop(0, i, body, 0)
    for p in range(n_pairs):
        o_ref[:, LANES * p:LANES * (p + 1)] = jnp.where(
            first_head, acc_ref[p, 0:tq, :], acc_ref[p, tq:2 * tq, :]).astype(o_ref.dtype)


def _attn_prompt(proj, kt_all, vt_all, bias, layer, batch, seq, tq):
    m = proj.shape[0]
    nq = seq // tq
    blk = SB_WIDTH
    kv_spec = pl.BlockSpec((None, None, blk, seq), lambda b, i: (layer, b, 0, 0))
    return pl.pallas_call(
        functools.partial(_attn_prompt_kernel, tq=tq),
        grid=(batch, nq),
        in_specs=[pl.BlockSpec(memory_space=pltpu.SMEM),
                  pl.BlockSpec((tq, blk), lambda b, i: (b * nq + i, COL_Q // blk)),
                  kv_spec, kv_spec,
                  pl.BlockSpec((tq, tq), lambda b, i: (0, 0))],
        out_specs=pl.BlockSpec((tq, blk), lambda b, i: (b * nq + i, 0)),
        out_shape=jax.ShapeDtypeStruct((m, blk), BF16),
        scratch_shapes=[pltpu.VMEM((nq, blk, tq), BF16),
                        pltpu.VMEM((nq, blk, tq), BF16),
                        pltpu.VMEM((SB_HEADS // 2, 2 * tq, LANES), BF16),
                        pltpu.VMEM((SB_HEADS // 2, 2 * tq, LANES), F32),
                        pltpu.VMEM((SB_HEADS // 2, 2 * tq, LANES), F32)],
        compiler_params=_cparams(("arbitrary", "arbitrary")),
        name="attn_prompt",
    )(bias, proj, kt_all, vt_all, _tri_incl(tq))


def _same_head(rows, n_new):
    rh = lax.broadcasted_iota(jnp.int32, (rows, SB_WIDTH), 0) >> (n_new.bit_length() - 1)
    ch = lax.broadcasted_iota(jnp.int32, (rows, SB_WIDTH), 1) >> (SB_HEAD_DIM.bit_length() - 1)
    return rh == ch


def _attn_sample_kernel(pt_ref, brow_ref, q_ref, kn_ref, vn_ref, tri_ref, *rest, pages_per_step, n_new):
    k_refs = rest[:pages_per_step]
    v_refs = rest[pages_per_step:2 * pages_per_step]
    o_ref = rest[2 * pages_per_step]
    qbd_ref, acc_ref, car_ref = rest[2 * pages_per_step + 1:]
    s = pl.program_id(1)
    rows = SB_HEADS * n_new
    page = tri_ref.shape[1]
    tri = tri_ref[...]
    brow = brow_ref[...]

    @pl.when(s == 0)
    def _():
        q = q_ref[...] * (1.0 / math.sqrt(SB_HEAD_DIM))
        qt = jnp.concatenate([q] * SB_HEADS, axis=0)
        qbd = jnp.where(_same_head(rows, n_new), qt, 0.0).astype(BF16)
        qbd_ref[...] = qbd
        kn = _pad_rows(kn_ref[...], page).astype(BF16)
        vn = _pad_rows(vn_ref[...], page).astype(BF16)
        qi = lax.broadcasted_iota(jnp.int32, (rows, page), 0) & (n_new - 1)
        kj = lax.broadcasted_iota(jnp.int32, (rows, page), 1)
        w, ssum = _sb_tile(_dot_nt(qbd, kn) + brow, 0.0, tri, kj < qi)
        acc_ref[...] = jnp.dot(w.astype(BF16), vn, preferred_element_type=F32)
        car_ref[...] = jnp.broadcast_to(ssum, car_ref.shape)

    qbd = qbd_ref[...]
    z = jnp.concatenate([jnp.dot(qbd, k_refs[j][...].astype(BF16), preferred_element_type=F32)
                         for j in range(pages_per_step)], axis=0)
    z = z + jnp.concatenate([brow] * pages_per_step, axis=0)

    sp = jnp.maximum(z, 0.0) + jnp.log(1.0 + jnp.exp(-jnp.abs(z)))
    cum = jnp.dot(sp.astype(BF16), tri, preferred_element_type=F32)
    ssum = jnp.sum(sp, axis=-1, keepdims=True)
    car = car_ref[...]
    cars = []
    for j in range(pages_per_step):
        cars.append(car)
        car = car + ssum[rows * j:rows * (j + 1)]
    car_ref[...] = car
    w = jnp.exp(z - cum - jnp.concatenate(cars, axis=0)).astype(BF16)
    acc = acc_ref[...]
    for j in range(pages_per_step):
        acc = acc + _dot_nt(w[rows * j:rows * (j + 1)], v_refs[j][...].astype(BF16))
    acc_ref[...] = acc

    @pl.when(s == pl.num_programs(1) - 1)
    def _():
        full = jnp.where(_same_head(rows, n_new), acc_ref[...], 0.0)
        out = full[0:n_new]
        for h in range(1, SB_HEADS):
            out = out + full[h * n_new:(h + 1) * n_new]
        o_ref[...] = out.astype(o_ref.dtype)


def _attn_sample(proj, k_new, v_new, bias, cache_k, cache_v, page_table, layer, batch, n_new, pages_per_step):
    page = cache_k.shape[3]
    n_pages = page_table.shape[0] // batch
    n_steps = n_pages // pages_per_step
    rows = SB_HEADS * n_new
    blk = SB_WIDTH
    tri = _tri_incl(page)
    brow = jnp.broadcast_to(jnp.repeat(bias, n_new)[:, None], (rows, page))

    def page_map(j):
        def index_map(b, s, pt):
            return (layer, pt[b * n_pages + (n_pages - 1 - (s * pages_per_step + j))], 0, 0)
        return index_map

    page_specs = [pl.BlockSpec((None, None, blk, page), page_map(j)) for j in range(pages_per_step)]
    const = lambda shape: pl.BlockSpec(shape, lambda b, s, pt: (0, 0))
    grid_spec = pltpu.PrefetchScalarGridSpec(
        num_scalar_prefetch=1,
        grid=(batch, n_steps),
        in_specs=[const((rows, page)),
                  pl.BlockSpec((n_new, blk), lambda b, s, pt: (b, COL_Q // blk)),
                  pl.BlockSpec((n_new, blk), lambda b, s, pt: (b, 0)),
                  pl.BlockSpec((n_new, blk), lambda b, s, pt: (b, 0)),
                  const((page, page))]
                 + page_specs + page_specs,
        out_specs=pl.BlockSpec((n_new, blk), lambda b, s, pt: (b, 0)),
        scratch_shapes=[pltpu.VMEM((rows, blk), BF16),
                        pltpu.VMEM((rows, blk), F32),
                        pltpu.VMEM((rows, page), F32)],
    )
    return pl.pallas_call(
        functools.partial(_attn_sample_kernel, pages_per_step=pages_per_step, n_new=n_new),
        grid_spec=grid_spec,
        out_shape=jax.ShapeDtypeStruct((batch * n_new, blk), F32),
        compiler_params=_cparams(("arbitrary", "arbitrary")),
        name="attn_sample",
    )(page_table, brow, proj, k_new, v_new, tri,
      *([cache_k] * pages_per_step), *([cache_v] * pages_per_step))


def _ssd_kernel(xbc_ref, z_ref, dt_ref, hist_ref, s0_ref, cw_ref, cb_ref, dtb_ref, alog_ref,
                dskip_ref, gain_ref, tri_ref, eexp_ref, y_ref, snew_ref, xh_scr, st_scr,
                *, rows_in):
    q = SSD_CHUNK
    c = pl.program_id(1)
    n_state = SSD_STATE
    gw = SSD_WIDTH // SSD_GROUPS

    @pl.when(c == 0)
    def _():
        xh_scr[0:SUBLANES, :] = hist_ref[...]
        for j in range(SSD_WIDTH // LANES):
            st_scr[:, LANES * j:LANES * (j + 1)] = s0_ref[LANES * j:LANES * (j + 1), :].T

    x = _pad_rows(xbc_ref[...], q)
    xh_scr[SUBLANES:SUBLANES + q, :] = x
    cw = cw_ref[...]
    conv = cb_ref[...] + x * cw[SSD_CONV - 1:SSD_CONV]
    for k in range(1, SSD_CONV):
        conv = conv + xh_scr[SUBLANES - k:SUBLANES - k + q, :] * cw[SSD_CONV - 1 - k:SSD_CONV - k]
    xh_scr[0:SUBLANES, :] = xh_scr[q:q + SUBLANES, :]
    xc = _silu(conv)
    xs = xc[:, :SSD_WIDTH]

    dt = _softplus(_pad_rows(dt_ref[...], q) + dtb_ref[...])
    if rows_in < q:
        rvalid = lax.broadcasted_iota(jnp.int32, (q, LANES), 0) < rows_in
        dt = jnp.where(rvalid, dt, 0.0)
    a = -jnp.exp(alog_ref[...])
    tri = tri_ref[...]
    cum = _dot_sel_lhs(tri, dt * a, 3)
    cum_t = cum.T
    dt_t = dt.T
    eexp = eexp_ref[...]
    cum_x = _dot_sel_rhs(cum, eexp, 3)
    dt_x = _dot_sel_rhs(dt, eexp, 3)
    cend_x = cum_x[q - 1:q, :]
    to_end_x = jnp.exp(cend_x - cum_x) * dt_x
    xw = (xs * to_end_x).astype(BF16)

    tt = lax.broadcasted_iota(jnp.int32, (q, q), 0)
    ss = lax.broadcasted_iota(jnp.int32, (q, q), 1)
    causal = ss <= tt
    lane = lax.broadcasted_iota(jnp.int32, (1, LANES), 1)
    first_head = lane < SSD_HEAD_DIM

    pairs_per_group = gw // LANES
    y_groups = []
    for g in range(SSD_GROUPS):
        bg = xc[:, SSD_WIDTH + g * n_state:SSD_WIDTH + (g + 1) * n_state]
        cg = xc[:, SSD_WIDTH + (SSD_GROUPS + g) * n_state:SSD_WIDTH + (SSD_GROUPS + g + 1) * n_state]
        cg16 = cg.astype(BF16)
        gcols = slice(gw * g, gw * (g + 1))
        y_off = jnp.dot(cg16, st_scr[:, gcols].astype(BF16), preferred_element_type=F32)
        cb = _dot_nt(cg16, bg.astype(BF16))
        y_pairs = []
        for jj in range(pairs_per_group):
            j = g * pairs_per_group + jj
            xs2 = xs[:, LANES * j:LANES * (j + 1)].astype(BF16)
            halves = []
            for hl in range(2):
                h = 2 * j + hl
                seg = cum[:, h:h + 1] - cum_t[h:h + 1, :]
                decay = jnp.exp(jnp.where(causal, seg, -1e30))
                wts = cb * decay * dt_t[h:h + 1, :]
                halves.append(jnp.dot(wts.astype(BF16), xs2, preferred_element_type=F32))
            y_pairs.append(jnp.where(first_head, halves[0], halves[1]))
        y_groups.append(jnp.concatenate(y_pairs, axis=1) + y_off * jnp.exp(cum_x[:, gcols]))
        st_scr[:, gcols] = (st_scr[:, gcols] * jnp.exp(cend_x[:, gcols])
                            + jnp.dot(bg.T.astype(BF16), xw[:, gcols], preferred_element_type=F32))

    y = jnp.concatenate(y_groups, axis=1) + dskip_ref[...] * xs
    y = _rms(y * _silu(_pad_rows(z_ref[...], q)), gain_ref[...])
    y_ref[...] = y[0:rows_in].astype(y_ref.dtype)

    @pl.when(c == pl.num_programs(1) - 1)
    def _():
        for j in range(SSD_WIDTH // LANES):
            snew_ref[LANES * j:LANES * (j + 1), :] = st_scr[:, LANES * j:LANES * (j + 1)].T


def _ssd(proj, dt_raw, hist, s0, conv_w, conv_b, dt_bias, a_log, d_skip, gain, batch, seq, out_dtype):
    q = SSD_CHUNK
    m = proj.shape[0]
    rows_in = q if seq % q == 0 else seq
    nc = seq // rows_in
    tri = (lax.broadcasted_iota(jnp.int32, (q, q), 1)
           <= lax.broadcasted_iota(jnp.int32, (q, q), 0)).astype(BF16)
    eexp = (lax.broadcasted_iota(jnp.int32, (LANES, SSD_WIDTH), 0)
            == lax.broadcasted_iota(jnp.int32, (LANES, SSD_WIDTH), 1) // SSD_HEAD_DIM).astype(BF16)
    pad_h = LANES - SSD_HEADS
    cw = jnp.pad(conv_w, ((0, SUBLANES - SSD_CONV), (0, 0)))
    const = lambda shape: pl.BlockSpec(shape, lambda b, c: (0,) * len(shape))
    return pl.pallas_call(
        functools.partial(_ssd_kernel, rows_in=rows_in),
        grid=(batch, nc),
        in_specs=[pl.BlockSpec((rows_in, SSD_CONV_DIM), lambda b, c: (b * nc + c, COL_XBC // SSD_CONV_DIM)),
                  pl.BlockSpec((rows_in, SSD_WIDTH), lambda b, c: (b * nc + c, COL_Z // SSD_WIDTH)),
                  pl.BlockSpec((rows_in, LANES), lambda b, c: (b * nc + c, 0)),
                  pl.BlockSpec((None, SUBLANES, SSD_CONV_DIM), lambda b, c: (b, 0, 0)),
                  pl.BlockSpec((None, SSD_WIDTH, SSD_STATE), lambda b, c: (b, 0, 0)),
                  const((SUBLANES, SSD_CONV_DIM)),
                  const((1, SSD_CONV_DIM)),
                  const((1, LANES)),
                  const((1, LANES)),
                  const((1, SSD_WIDTH)),
                  const((1, SSD_WIDTH)),
                  const((q, q)),
                  const((LANES, SSD_WIDTH))],
        out_specs=[pl.BlockSpec((rows_in, SSD_WIDTH), lambda b, c: (b * nc + c, 0)),
                   pl.BlockSpec((None, SSD_WIDTH, SSD_STATE), lambda b, c: (b, 0, 0))],
        out_shape=[jax.ShapeDtypeStruct((m, SSD_WIDTH), out_dtype),
                   jax.ShapeDtypeStruct((batch, SSD_WIDTH, SSD_STATE), F32)],
        scratch_shapes=[pltpu.VMEM((SUBLANES + q + SUBLANES, SSD_CONV_DIM), F32),
                        pltpu.VMEM((SSD_STATE, SSD_WIDTH), F32)],
        compiler_params=_cparams(("parallel", "arbitrary")),
        name="ssd",
    )(proj, proj, dt_raw, hist, s0, cw, conv_b.reshape(1, -1),
      jnp.pad(dt_bias, (0, pad_h)).reshape(1, LANES), jnp.pad(a_log, (0, pad_h)).reshape(1, LANES),
      jnp.repeat(d_skip, SSD_HEAD_DIM).reshape(1, SSD_WIDTH), gain.reshape(1, SSD_WIDTH), tri, eexp)


def _pool_kernel(x_ref, hist_ref, w_ref, scale_ref, y_ref, buf, *, tm, pos0, carry):
    hist_rows = 2 * SUBLANES
    t = pl.program_id(1)

    @pl.when(t == 0)
    def _():
        buf[0:hist_rows, :] = hist_ref[...]

    x = x_ref[...]
    buf[hist_rows:hist_rows + tm, :] = x
    pos = pos0 + t * tm + lax.broadcasted_iota(jnp.int32, (tm, 1), 0)
    for g, win in enumerate(POOL_WINDOWS):
        cols = slice(POOL_GROUP_DIM * g, POOL_GROUP_DIM * (g + 1))
        xg = x[:, cols]
        wsum = xg
        for k in range(1, win):
            wsum = wsum + buf[hist_rows - k:hist_rows - k + tm, cols]
        count = jnp.minimum(win, pos + 1).astype(F32)
        diff = wsum / count - xg
        yg = jnp.dot(diff.astype(BF16), w_ref[g], preferred_element_type=F32) * scale_ref[:, cols]
        y_ref[:, cols] = yg.astype(y_ref.dtype)
    if carry:
        buf[0:hist_rows, :] = buf[tm:tm + hist_rows, :]


def _pool(proj, hist, w, scale, batch, seq, tm, pos0, out_dtype):
    m = proj.shape[0]
    nt = seq // tm
    blk = POOL_WIDTH
    return pl.pallas_call(
        functools.partial(_pool_kernel, tm=tm, pos0=pos0, carry=nt > 1),
        grid=(batch, nt),
        in_specs=[pl.BlockSpec((tm, blk), lambda b, t: (b * nt + t, COL_POOL // blk)),
                  pl.BlockSpec((None, 2 * SUBLANES, blk), lambda b, t: (b, 0, 0)),
                  pl.BlockSpec((len(POOL_WINDOWS), POOL_GROUP_DIM, POOL_GROUP_DIM), lambda b, t: (0, 0, 0)),
                  pl.BlockSpec((1, blk), lambda b, t: (0, 0))],
        out_specs=pl.BlockSpec((tm, blk), lambda b, t: (b * nt + t, 0)),
        out_shape=jax.ShapeDtypeStruct((m, blk), out_dtype),
        scratch_shapes=[pltpu.VMEM((2 * SUBLANES + tm, blk), F32)],
        compiler_params=_cparams(("parallel", "arbitrary")),
        name="pool",
    )(proj, hist, w, scale.reshape(1, blk))


def _outproj_kernel(o_ref, y_ref, p_ref, x_ref, w_ref, g_ref, xn_ref, h_ref, *, n_sub):
    a, b = SB_WIDTH, SB_WIDTH + SSD_WIDTH
    rows = x_ref.shape[0] // n_sub
    accs = []
    for s in range(n_sub):
        r = slice(rows * s, rows * (s + 1))
        acc = x_ref[r, :]
        acc = acc + jnp.dot(o_ref[r, :].astype(BF16), w_ref[0:a, :], preferred_element_type=F32)
        acc = acc + jnp.dot(y_ref[r, :].astype(BF16), w_ref[a:b, :], preferred_element_type=F32)
        acc = acc + jnp.dot(p_ref[r, :].astype(BF16), w_ref[b:, :], preferred_element_type=F32)
        accs.append(acc)
    for s, acc in enumerate(accs):
        r = slice(rows * s, rows * (s + 1))
        xn_ref[r, :] = acc
        h_ref[r, :] = _rms(acc, g_ref[...]).astype(h_ref.dtype)


def _outproj(o, y, p, x, w, layer, gain, tm):
    m, d = x.shape
    row = lambda width: pl.BlockSpec((tm, width), lambda i: (i, 0))
    return pl.pallas_call(
        functools.partial(_outproj_kernel, n_sub=2 if tm >= 512 else 1),
        grid=(m // tm,),
        in_specs=[row(SB_WIDTH), row(SSD_WIDTH), row(POOL_WIDTH), row(d),
                  pl.BlockSpec((None,) + w.shape[1:], lambda i: (layer, 0, 0)),
                  pl.BlockSpec((1, d), lambda i: (0, 0))],
        out_specs=[row(d), row(d)],
        out_shape=[jax.ShapeDtypeStruct((m, d), F32), jax.ShapeDtypeStruct((m, d), BF16)],
        compiler_params=_cparams(("parallel",)),
        name="outproj",
    )(o, y, p, x, w, gain.reshape(1, d))


def _ffn_conv_rows(buf, gs, cw, cb):
    rows = gs.shape[0]
    buf[SUBLANES:SUBLANES + rows, :] = gs
    out = cb + gs * cw[FFN_CONV - 1:FFN_CONV]
    for k in range(1, FFN_CONV):
        out = out + buf[SUBLANES - k:SUBLANES - k + rows, :] * cw[FFN_CONV - 1 - k:FFN_CONV - k]
    return out


def _ffn_up_seqs_kernel(h_ref, wg_ref, wu_ref, hist_ref, cw_ref, cb_ref, act_ref, tail_ref, buf,
                        *, seg_rows, n_seg):
    h = h_ref[...]
    g = jnp.dot(h, wg_ref[...].astype(BF16), preferred_element_type=F32)
    u = jnp.dot(h, wu_ref[...].astype(BF16), preferred_element_type=F32)
    for s in range(n_seg):
        rows = slice(seg_rows * s, seg_rows * (s + 1))
        buf[0:SUBLANES, :] = hist_ref[s]
        gc = _ffn_conv_rows(buf, g[rows], cw_ref[...], cb_ref[...])
        act_ref[rows, :] = (_silu(gc) * u[rows]).astype(act_ref.dtype)
    tail_ref[...] = g


def _ffn_up_stream_kernel(h_ref, wg_ref, wu_ref, hist_ref, cw_ref, cb_ref, act_ref, tail_ref, buf, wg16, wu16,
                          *, tiles_per_seq, n_sub):
    i = pl.program_id(1)

    @pl.when(i == 0)
    def _():
        wg16[...] = wg_ref[...].astype(BF16)
        wu16[...] = wu_ref[...].astype(BF16)

    @pl.when(i % tiles_per_seq == 0)
    def _():
        buf[0:SUBLANES, :] = hist_ref[...]

    rows = h_ref.shape[0] // n_sub
    gu = []
    for s in range(n_sub):
        h = h_ref[rows * s:rows * (s + 1), :]
        gu.append((jnp.dot(h, wg16[...], preferred_element_type=F32),
                   jnp.dot(h, wu16[...], preferred_element_type=F32)))
    for s, (g, u) in enumerate(gu):
        gc = _ffn_conv_rows(buf, g, cw_ref[...], cb_ref[...])
        buf[0:SUBLANES, :] = buf[rows:rows + SUBLANES, :]
        act_ref[rows * s:rows * (s + 1), :] = (_silu(gc) * u).astype(act_ref.dtype)
    tail_ref[...] = gu[-1][0][rows - SUBLANES:rows]


def _ffn_up(h, wg, wu, layer, hist, conv_w, conv_b, batch, seq, tm, tn):
    m, d = h.shape
    dff = wg.shape[2]
    nt = m // tm
    nj = dff // tn
    cw = jnp.pad(conv_w, ((0, SUBLANES - FFN_CONV), (0, 0)))
    args = (h, wg, wu, hist, cw, conv_b.reshape(1, dff))
    if tm > seq:
        n_seg = tm // seq
        return pl.pallas_call(
            functools.partial(_ffn_up_seqs_kernel, seg_rows=seq, n_seg=n_seg),
            grid=(nj, nt),
            in_specs=[pl.BlockSpec((tm, d), lambda j, i: (i, 0)),
                      pl.BlockSpec((None, d, tn), lambda j, i: (layer, 0, j)),
                      pl.BlockSpec((None, d, tn), lambda j, i: (layer, 0, j)),
                      pl.BlockSpec((n_seg, SUBLANES, tn), lambda j, i: (i, 0, j)),
                      pl.BlockSpec((SUBLANES, tn), lambda j, i: (0, j)),
                      pl.BlockSpec((1, tn), lambda j, i: (0, j))],
            out_specs=[pl.BlockSpec((tm, tn), lambda j, i: (i, j)),
                       pl.BlockSpec((tm, tn), lambda j, i: (i, j))],
            out_shape=[jax.ShapeDtypeStruct((m, dff), BF16), jax.ShapeDtypeStruct((m, dff), F32)],
            scratch_shapes=[pltpu.VMEM((SUBLANES + seq + SUBLANES, tn), F32)],
            compiler_params=_cparams(("arbitrary", "arbitrary")),
            name="ffn_up",
        )(*args)

    tps = seq // tm
    return pl.pallas_call(
        functools.partial(_ffn_up_stream_kernel, tiles_per_seq=tps, n_sub=4 if tm >= 1024 else 1),
        grid=(nj, nt),
        in_specs=[pl.BlockSpec((tm, d), lambda j, i: (i, 0)),
                  pl.BlockSpec((None, d, tn), lambda j, i: (layer, 0, j)),
                  pl.BlockSpec((None, d, tn), lambda j, i: (layer, 0, j)),
                  pl.BlockSpec((None, SUBLANES, tn), lambda j, i: (i // tps, 0, j)),
                  pl.BlockSpec((SUBLANES, tn), lambda j, i: (0, j)),
                  pl.BlockSpec((1, tn), lambda j, i: (0, j))],
        out_specs=[pl.BlockSpec((tm, tn), lambda j, i: (i, j)),
                   pl.BlockSpec((None, SUBLANES, tn), lambda j, i: (i, 0, j))],
        out_shape=[jax.ShapeDtypeStruct((m, dff), BF16), jax.ShapeDtypeStruct((nt, SUBLANES, dff), F32)],
        scratch_shapes=[pltpu.VMEM((SUBLANES + tm + SUBLANES, tn), F32),
                        pltpu.VMEM((d, tn), BF16),
                        pltpu.VMEM((d, tn), BF16)],
        compiler_params=_cparams(("arbitrary", "arbitrary")),
        name="ffn_up",
    )(*args)


def _ffn_down_kernel(a_ref, w_ref, x_ref, xn_ref, *, n_sub):
    rows = a_ref.shape[0] // n_sub
    parts = [jnp.dot(a_ref[rows * s:rows * (s + 1), :], w_ref[...], preferred_element_type=F32)
             for s in range(n_sub)]
    for s, part in enumerate(parts):
        r = slice(rows * s, rows * (s + 1))
        xn_ref[r, :] = x_ref[r, :] + part


def _ffn_down(act, w, layer, x, tm, tn):
    m, d = x.shape
    dff = act.shape[1]
    return pl.pallas_call(
        functools.partial(_ffn_down_kernel, n_sub=2 if tm >= 1024 else 1),
        grid=(m // tm, d // tn),
        in_specs=[pl.BlockSpec((tm, dff), lambda i, j: (i, 0)),
                  pl.BlockSpec((None, dff, tn), lambda i, j: (layer, 0, j)),
                  pl.BlockSpec((tm, tn), lambda i, j: (i, j))],
        out_specs=pl.BlockSpec((tm, tn), lambda i, j: (i, j)),
        out_shape=jax.ShapeDtypeStruct((m, d), F32),
        compiler_params=_cparams(("arbitrary", "arbitrary")),
        name="ffn_down",
    )(act, w, x)


def _tiles(m):
    if m >= 1024:
        return dict(inproj=1024, outproj=512, ffn_up=1024, ffn_down=1024, norm=512)
    return dict(inproj=m, outproj=m, ffn_up=m, ffn_down=m, norm=m)


def _front_pad_rows(a, rows):
    return jnp.pad(a, ((0, 0), (rows - a.shape[1], 0), (0, 0)))


def _run_trunk(x, pos0, cache, states, wts, final_norm):
    batch, seq, d = x.shape
    m = batch * seq
    depth = wts["w_main"].shape[0]
    tiles = _tiles(m)
    x2 = x.reshape(m, d)
    small = seq % SSD_CHUNK != 0
    mix_dtype = F32 if small else BF16
    outs = [[] for _ in range(4)]
    kv_transposed = cache is None
    kv_all_shape = (depth, batch, SB_WIDTH, seq) if kv_transposed else (depth, m, SB_HEADS, SB_HEAD_DIM)
    k_all = jnp.zeros(kv_all_shape, F32)
    v_all = jnp.zeros(kv_all_shape, F32)
    for l in range(depth):
        res = _inproj(x2, wts["norm1"][l], wts["w_main"], wts["w_dt"], wts["w_k"], wts["w_v"], k_all, v_all, l,
                      tiles["inproj"], seq, kv_transposed)
        proj, dt_raw, k_all, v_all = res[0], res[1], res[-2], res[-1]
        last_rows = proj.reshape(batch, seq, PROJ_WIDTH)[:, max(0, seq - POOL_HIST):]
        xbc = last_rows[:, :, COL_XBC:COL_XBC + SSD_CONV_DIM]
        xp = last_rows[:, :, COL_POOL:COL_POOL + POOL_WIDTH]
        if cache is None:
            ssm0 = jnp.zeros((batch, SSD_WIDTH, SSD_STATE), F32)
            conv_hist = jnp.zeros((batch, SSD_CONV - 1, SSD_CONV_DIM), F32)
            pool_hist = jnp.zeros((batch, POOL_HIST, POOL_WIDTH), F32)
            ffn_hist = jnp.zeros((batch, FFN_CONV - 1, wts["w_gate"].shape[2]), F32)
            o_sb = _attn_prompt(proj, k_all, v_all, wts["sb_bias"][l], l, batch, seq, 256)
        else:
            cache_k, cache_v, page_table = cache
            ssm0 = states[0][l].reshape(batch, SSD_WIDTH, SSD_STATE)
            conv_hist, pool_hist, ffn_hist = states[1][l], states[2][l], states[3][l]
            o_sb = _attn_sample(proj, res[2], res[3], wts["sb_bias"][l], cache_k, cache_v, page_table,
                                l, batch, seq, 16)

        y_ssd, ssm_new = _ssd(proj, dt_raw, _front_pad_rows(conv_hist, SUBLANES), ssm0,
                              wts["ssd_conv_w"][l], wts["ssd_conv_b"][l], wts["dt_bias"][l], wts["a_log"][l],
                              wts["d_skip"][l], wts["ssd_norm"][l], batch, seq, mix_dtype)
        y_pool = _pool(proj, _front_pad_rows(pool_hist, 2 * SUBLANES), wts["pool_w"][l], wts["pool_scale"][l],
                       batch, seq, min(seq, 512), pos0, mix_dtype)
        outs[0].append(ssm_new.reshape(batch, SSD_HEADS, SSD_HEAD_DIM, SSD_STATE))
        outs[1].append(_tail_rows(conv_hist, xbc, SSD_CONV - 1))
        outs[2].append(_tail_rows(pool_hist, xp, POOL_HIST))

        x2, h2 = _outproj(o_sb, y_ssd, y_pool, x2, wts["w_out"], l, wts["norm2"][l], tiles["outproj"])
        act, tail = _ffn_up(h2, wts["w_gate"], wts["w_up"], l, _front_pad_rows(ffn_hist, SUBLANES),
                            wts["ffn_conv_w"][l], wts["ffn_conv_b"][l], batch, seq, tiles["ffn_up"], 512)
        dff = act.shape[1]
        if tiles["ffn_up"] <= seq:
            tps = seq // tiles["ffn_up"]
            g_last = tail.reshape(batch, tps, SUBLANES, dff)[:, tps - 1, SUBLANES - (FFN_CONV - 1):]
        else:
            g_last = tail.reshape(batch, seq, dff)[:, seq - (FFN_CONV - 1):]
        outs[3].append(g_last)
        x2 = _ffn_down(act, wts["w_down"], l, x2, tiles["ffn_down"], 512)
    y_final = _rmsnorm(x2, final_norm, tiles["norm"], F32).reshape(batch, seq, d)
    if kv_transposed:
        to_cache = lambda a: a.reshape(depth, batch, SB_HEADS, SB_HEAD_DIM, seq).transpose(0, 1, 4, 2, 3)
    else:
        to_cache = lambda a: a.reshape(depth, batch, seq, SB_HEADS, SB_HEAD_DIM)
    return y_final, [to_cache(k_all), to_cache(v_all)] + [jnp.stack(o) for o in outs]


def kernel(x_prompt, x_sample, cache_k, cache_v, page_table, state_ssm, state_ssm_conv, state_pool, state_ffn_conv, norm1, w_in, sb_bias, ssd_conv_w, ssd_conv_b, dt_bias, a_log, d_skip, ssd_norm, pool_w, pool_scale, w_out, norm2, w_gate, w_up, ffn_conv_w, ffn_conv_b, w_down, final_norm):
    c_q, c_z = 0, 3 * SB_WIDTH
    c_xbc = c_z + SSD_WIDTH
    c_dt = c_xbc + SSD_CONV_DIM
    c_pool = c_dt + SSD_HEADS
    c_k, c_v = c_q + SB_WIDTH, c_q + 2 * SB_WIDTH
    w_t = jnp.swapaxes(w_in, 1, 2)
    w_main = jnp.concatenate([w_t[:, c_xbc:c_dt], w_t[:, c_q:c_k], w_t[:, c_z:c_xbc], w_t[:, c_pool:]],
                             axis=1).astype(BF16)
    w_dt = jnp.pad(w_t[:, c_dt:c_pool], ((0, 0), (0, LANES - SSD_HEADS), (0, 0))).astype(BF16)
    wts = dict(norm1=norm1, w_main=w_main, w_dt=w_dt, w_k=w_t[:, c_k:c_v].astype(BF16),
               w_v=w_t[:, c_v:c_z].astype(BF16), sb_bias=sb_bias, ssd_conv_w=ssd_conv_w,
               ssd_conv_b=ssd_conv_b, dt_bias=dt_bias, a_log=a_log, d_skip=d_skip, ssd_norm=ssd_norm,
               pool_w=pool_w.astype(BF16), pool_scale=pool_scale, w_out=w_out.astype(BF16), norm2=norm2,
               w_gate=w_gate, w_up=w_up, ffn_conv_w=ffn_conv_w,
               ffn_conv_b=ffn_conv_b, w_down=w_down.astype(BF16))

    y_p, st_p = _run_trunk(x_prompt, 0, None, None, wts, final_norm)

    depth, n_pool, page = cache_k.shape[:3]
    past_len = page_table.shape[1] * page
    paged = lambda c: c.transpose(0, 1, 3, 4, 2).reshape(depth, n_pool, SB_WIDTH, page)
    cache = (paged(cache_k), paged(cache_v), page_table.reshape(-1))
    y_s, st_s = _run_trunk(x_sample, past_len, cache,
                           (state_ssm, state_ssm_conv, state_pool, state_ffn_conv), wts, final_norm)
    return (y_p, y_s, *st_p, *st_s)
```
